```python
import math
import jax, jax.numpy as jnp
from jax import lax
import numpy as np

D_MODEL = 1024
BATCH = 4
SEQ = 4096
DEPTH = 1
DEC_BATCH = 16
DEC_SEQ = 16
PAST_LEN = 2048

CHUNK = 64
D_CONV = D_MODEL
CONV_A_WIDTH = 31
SSM_EXPAND = 2
D_INNER = SSM_EXPAND * D_MODEL
SSM_HEAD_DIM = 64
N_SSM_HEADS = D_INNER // SSM_HEAD_DIM
N_GROUPS = 4
D_STATE = 128
CONV_B_WIDTH = 4
D_XBC = D_INNER + 2 * N_GROUPS * D_STATE
D_FF = 2816
N_BRANCHES = 2
SPLIT_IDX = (2 * D_CONV,
             2 * D_CONV + D_INNER,
             2 * D_CONV + D_INNER + D_XBC,
             2 * D_CONV + D_INNER + D_XBC + N_SSM_HEADS)
D_IN_PROJ = SPLIT_IDX[3] + N_BRANCHES * D_MODEL
RMS_EPS = 1e-6
LN_EPS = 1e-5
FFN_RES_WEIGHT = 0.5

kernel_name = "hybrid_conformer_mamba2_stream_step"


def rms_norm(x, g):
    xf = x.astype(jnp.float32)
    y = xf * lax.rsqrt(jnp.mean(xf * xf, axis=-1, keepdims=True) + RMS_EPS)
    return (y * g.astype(jnp.float32)).astype(x.dtype)


def layer_norm(x, g, b):
    xf = x.astype(jnp.float32)
    mu = jnp.mean(xf, axis=-1, keepdims=True)
    xc = xf - mu
    y = xc * lax.rsqrt(jnp.mean(xc * xc, axis=-1, keepdims=True) + LN_EPS)
    return (y * g.astype(jnp.float32) + b.astype(jnp.float32)).astype(x.dtype)


def swiglu_ffn(x, w_gate, w_up, w_down):
    return (jax.nn.silu(x @ w_gate) * (x @ w_up)) @ w_down


def causal_depthwise_conv(u, buf, w, b):
    k = w.shape[0]
    ext = jnp.concatenate([buf.astype(u.dtype), u], axis=1)
    out = lax.conv_general_dilated(ext, w[:, None, :].astype(u.dtype), window_strides=(1,),
                                   padding='VALID', dimension_numbers=('NWC', 'WIO', 'NWC'),
                                   feature_group_count=u.shape[-1])
    return out + b.astype(u.dtype), ext[:, ext.shape[1] - (k - 1):]


def ssd_scan(xh, dt, A, Bm, Cm, h0):
    b, L, H, P = xh.shape
    G, N = Bm.shape[2], Bm.shape[3]
    R = H // G
    q = min(CHUNK, L)
    nc = L // q
    f32 = jnp.float32
    xdt = (xh.astype(f32) * dt[..., None]).reshape(b, nc, q, G, R, P)
    acum = jnp.cumsum((dt * A).reshape(b, nc, q, G, R), axis=2)
    Bc = Bm.astype(f32).reshape(b, nc, q, G, N)
    Cc = Cm.astype(f32).reshape(b, nc, q, G, N)
    causal = jnp.tril(jnp.ones((q, q), dtype=bool))

    def step(h, inp):
        xdt_c, B_c, C_c, acum_c = inp
        ac = jnp.moveaxis(acum_c, 1, -1)
        seg = ac[..., :, None] - ac[..., None, :]
        decay = jnp.exp(jnp.where(causal, seg, -jnp.inf))
        cb = jnp.einsum('bign,bjgn->bgij', C_c, B_c)
        y_in = jnp.einsum('bgij,bgrij,bjgrp->bigrp', cb, decay, xdt_c)
        y_st = jnp.einsum('bign,bigr,bgrpn->bigrp', C_c, jnp.exp(acum_c), h)
        tail = jnp.exp(acum_c[:, -1:] - acum_c)
        h_new = (jnp.exp(acum_c[:, -1])[..., None, None] * h
                 + jnp.einsum('bjgn,bjgr,bjgrp->bgrpn', B_c, tail, xdt_c))
        return h_new, y_in + y_st

    h = h0.astype(f32).reshape(b, G, R, P, N)
    inputs = (jnp.moveaxis(xdt, 1, 0), jnp.moveaxis(Bc, 1, 0), jnp.moveaxis(Cc, 1, 0), jnp.moveaxis(acum, 1, 0))
    h_fin, ys = lax.scan(step, h, inputs)
    y = jnp.moveaxis(ys, 0, 1).reshape(b, L, H, P)
    return y, h_fin.reshape(b, H, P, N)


def conformer_conv_branch(glu_in, conv_buf, w_dw, b_dw, ln_g, ln_b, w_pw):
    u = glu_in[..., :D_CONV] * jax.nn.sigmoid(glu_in[..., D_CONV:])
    v, new_buf = causal_depthwise_conv(u, conv_buf, w_dw, b_dw)
    v = jax.nn.silu(layer_norm(v, ln_g, ln_b))
    return v @ w_pw, new_buf


def mamba2_branch(z, xbc, dt_raw, conv_buf, ssm_state, w_conv, b_conv, dt_bias, a_log, d_skip, norm_g, w_out):
    f32 = jnp.float32
    xbc_c, new_buf = causal_depthwise_conv(xbc, conv_buf, w_conv, b_conv)
    xbc_c = jax.nn.silu(xbc_c)
    xs, Bm, Cm = jnp.split(xbc_c, [D_INNER, D_INNER + N_GROUPS * D_STATE], axis=-1)
    b, L, _ = xs.shape
    xh = xs.reshape(b, L, N_SSM_HEADS, SSM_HEAD_DIM)
    Bm = Bm.reshape(b, L, N_GROUPS, D_STATE)
    Cm = Cm.reshape(b, L, N_GROUPS, D_STATE)
    dt = jax.nn.softplus(dt_raw.astype(f32) + dt_bias.astype(f32))
    A = -jnp.exp(a_log.astype(f32))
    y, h_fin = ssd_scan(xh, dt, A, Bm, Cm, ssm_state)
    y = (y + d_skip.astype(f32)[:, None] * xh.astype(f32)).reshape(b, L, D_INNER)
    g = (y * jax.nn.silu(z.astype(f32))).reshape(b, L, N_GROUPS, D_INNER // N_GROUPS)
    g = g * lax.rsqrt(jnp.mean(g * g, axis=-1, keepdims=True) + RMS_EPS)
    g = g.reshape(b, L, D_INNER) * norm_g.astype(f32)
    return g.astype(z.dtype) @ w_out, new_buf, h_fin.astype(z.dtype)


def encoder_forward(x, st_a, st_b, st_h, p):
    new_a, new_b, new_h = [], [], []
    for l in range(DEPTH):
        x = x + FFN_RES_WEIGHT * swiglu_ffn(rms_norm(x, p['norm_ffn1'][l]), p['ffn1_w_gate'][l],
                                            p['ffn1_w_up'][l], p['ffn1_w_down'][l])
        h = rms_norm(x, p['norm_mix'][l])
        proj = h @ p['w_in'][l]
        glu_in, z, xbc, dt_raw, gate_logits = jnp.split(proj, list(SPLIT_IDX), axis=-1)
        y_a, buf_a = conformer_conv_branch(glu_in, st_a[l], p['conv_a_w'][l], p['conv_a_b'][l],
                                           p['ln_conv_g'][l], p['ln_conv_b'][l], p['w_conv_a_out'][l])
        y_b, buf_b, h_fin = mamba2_branch(z, xbc, dt_raw, st_b[l], st_h[l], p['conv_b_w'][l], p['conv_b_b'][l],
                                          p['dt_bias'][l], p['a_log'][l], p['d_skip'][l],
                                          p['ssm_norm_g'][l], p['w_ssm_out'][l])
        gates = jax.nn.sigmoid(gate_logits.astype(jnp.float32))
        g_a, g_b = gates[..., :D_MODEL], gates[..., D_MODEL:]
        merged = (g_a * y_a.astype(jnp.float32) + g_b * y_b.astype(jnp.float32)).astype(x.dtype)
        x = x + merged @ p['w_out'][l]
        x = x + FFN_RES_WEIGHT * swiglu_ffn(rms_norm(x, p['norm_ffn2'][l]), p['ffn2_w_gate'][l],
                                            p['ffn2_w_up'][l], p['ffn2_w_down'][l])
        new_a.append(buf_a)
        new_b.append(buf_b)
        new_h.append(h_fin)
    y = rms_norm(x, p['norm_final'])
    return y, jnp.stack(new_a), jnp.stack(new_b), jnp.stack(new_h)


def setup_inputs(seed: int = 0) -> dict:
    key = jax.random.key(seed)
    ks = jax.random.split(key, 32)
    f32 = jnp.float32

    def nrm(k, shape, scale):
        return jax.random.normal(k, shape, f32) * scale

    def gain(k, shape):
        return 1.0 + 0.05 * jax.random.normal(k, shape, f32)

    dt0 = jnp.exp(jax.random.uniform(ks[20], (DEPTH, N_SSM_HEADS), f32)
                  * (math.log(0.1) - math.log(0.001)) + math.log(0.001))
    dt_bias = dt0 + jnp.log(-jnp.expm1(-dt0))
    return {
        "x_prompt": jax.random.normal(ks[0], (BATCH, SEQ, D_MODEL), f32),
        "x_sample": jax.random.normal(ks[1], (DEC_BATCH, DEC_SEQ, D_MODEL), f32),
        "state_conv_a": nrm(ks[2], (DEPTH, DEC_BATCH, CONV_A_WIDTH - 1, D_CONV), 0.5),
        "state_conv_b": nrm(ks[3], (DEPTH, DEC_BATCH, CONV_B_WIDTH - 1, D_XBC), 1.0),
        "state_ssm": nrm(ks[4], (DEPTH, DEC_BATCH, N_SSM_HEADS, SSM_HEAD_DIM, D_STATE), 0.5),
        "norm_ffn1": gain(ks[5], (DEPTH, D_MODEL)),
        "ffn1_w_gate": nrm(ks[6], (DEPTH, D_MODEL, D_FF), D_MODEL ** -0.5),
        "ffn1_w_up": nrm(ks[7], (DEPTH, D_MODEL, D_FF), D_MODEL ** -0.5),
        "ffn1_w_down": nrm(ks[8], (DEPTH, D_FF, D_MODEL), D_FF ** -0.5),
        "norm_mix": gain(ks[9], (DEPTH, D_MODEL)),
        "w_in": nrm(ks[10], (DEPTH, D_MODEL, D_IN_PROJ), D_MODEL ** -0.5),
        "conv_a_w": nrm(ks[11], (DEPTH, CONV_A_WIDTH, D_CONV), CONV_A_WIDTH ** -0.5),
        "conv_a_b": nrm(ks[12], (DEPTH, D_CONV), 0.01),
        "ln_conv_g": gain(ks[13], (DEPTH, D_CONV)),
        "ln_conv_b": nrm(ks[14], (DEPTH, D_CONV), 0.01),
        "w_conv_a_out": nrm(ks[15], (DEPTH, D_CONV, D_MODEL), D_CONV ** -0.5),
        "conv_b_w": nrm(ks[16], (DEPTH, CONV_B_WIDTH, D_XBC), CONV_B_WIDTH ** -0.5),
        "conv_b_b": nrm(ks[17], (DEPTH, D_XBC), 0.01),
        "dt_bias": dt_bias,
        "a_log": jnp.log(jax.random.uniform(ks[18], (DEPTH, N_SSM_HEADS), f32, 1.0, 16.0)),
        "d_skip": gain(ks[19], (DEPTH, N_SSM_HEADS)),
        "ssm_norm_g": gain(ks[21], (DEPTH, D_INNER)),
        "w_ssm_out": nrm(ks[22], (DEPTH, D_INNER, D_MODEL), D_INNER ** -0.5),
        "w_out": nrm(ks[23], (DEPTH, D_MODEL, D_MODEL), D_MODEL ** -0.5),
        "norm_ffn2": gain(ks[24], (DEPTH, D_MODEL)),
        "ffn2_w_gate": nrm(ks[25], (DEPTH, D_MODEL, D_FF), D_MODEL ** -0.5),
        "ffn2_w_up": nrm(ks[26], (DEPTH, D_MODEL, D_FF), D_MODEL ** -0.5),
        "ffn2_w_down": nrm(ks[27], (DEPTH, D_FF, D_MODEL), D_FF ** -0.5),
        "norm_final": gain(ks[28], (D_MODEL,)),
    }


def reference(x_prompt, x_sample, state_conv_a, state_conv_b, state_ssm,
              norm_ffn1, ffn1_w_gate, ffn1_w_up, ffn1_w_down, norm_mix, w_in,
              conv_a_w, conv_a_b, ln_conv_g, ln_conv_b, w_conv_a_out,
              conv_b_w, conv_b_b, dt_bias, a_log, d_skip, ssm_norm_g, w_ssm_out,
              w_out, norm_ffn2, ffn2_w_gate, ffn2_w_up, ffn2_w_down, norm_final):
    p = dict(norm_ffn1=norm_ffn1, ffn1_w_gate=ffn1_w_gate, ffn1_w_up=ffn1_w_up, ffn1_w_down=ffn1_w_down,
             norm_mix=norm_mix, w_in=w_in, conv_a_w=conv_a_w, conv_a_b=conv_a_b, ln_conv_g=ln_conv_g,
             ln_conv_b=ln_conv_b, w_conv_a_out=w_conv_a_out, conv_b_w=conv_b_w, conv_b_b=conv_b_b,
             dt_bias=dt_bias, a_log=a_log, d_skip=d_skip, ssm_norm_g=ssm_norm_g, w_ssm_out=w_ssm_out,
             w_out=w_out, norm_ffn2=norm_ffn2, ffn2_w_gate=ffn2_w_gate, ffn2_w_up=ffn2_w_up,
             ffn2_w_down=ffn2_w_down, norm_final=norm_final)
    bp = x_prompt.shape[0]
    dtp = x_prompt.dtype
    zero_a = jnp.zeros((DEPTH, bp, CONV_A_WIDTH - 1, D_CONV), dtp)
    zero_b = jnp.zeros((DEPTH, bp, CONV_B_WIDTH - 1, D_XBC), dtp)
    zero_h = jnp.zeros((DEPTH, bp, N_SSM_HEADS, SSM_HEAD_DIM, D_STATE), dtp)
    y_prompt, new_conv_a_prompt, new_conv_b_prompt, new_ssm_prompt = encoder_forward(
        x_prompt, zero_a, zero_b, zero_h, p)
    y_sample, new_conv_a_sample, new_conv_b_sample, new_ssm_sample = encoder_forward(
        x_sample, state_conv_a, state_conv_b, state_ssm, p)
    return (y_prompt, y_sample, new_conv_a_prompt, new_conv_b_prompt, new_ssm_prompt,
            new_conv_a_sample, new_conv_b_sample, new_ssm_sample)
```

```python
import functools

import jax
import jax.numpy as jnp
from jax import lax
from jax.experimental import pallas as pl
from jax.experimental.pallas import tpu as pltpu

D_MODEL = 1024
D_FF = 2816
D_CONV = D_MODEL
CONV_A_WIDTH = 31
D_INNER = 2048
HEAD_DIM = 64
N_HEADS = D_INNER // HEAD_DIM
N_GROUPS = 4
HEADS_PER_GROUP = N_HEADS // N_GROUPS
GROUP_WIDTH = D_INNER // N_GROUPS
D_STATE = 128
CONV_B_WIDTH = 4
D_XBC = D_INNER + 2 * N_GROUPS * D_STATE
RMS_EPS = 1e-6
LN_EPS = 1e-5
FFN_RES_WEIGHT = 0.5

LANES = 128
SUBLANES = 8
CONV_A_PAD = 32
CONV_B_PAD = 8
SSD_CHUNK = 128
FF_CHUNK = 256
VMEM_LIMIT = 56 * 1024 * 1024
NEG_BIG = -1e30

F32 = jnp.float32
BF16 = jnp.bfloat16


def _dot(a, b):
    return jnp.dot(a, b, preferred_element_type=F32)


def _rms_norm(x, g):
    ms = jnp.mean(x * x, axis=-1, keepdims=True)
    return x * lax.rsqrt(ms + RMS_EPS) * g


def _whole():
    return pl.BlockSpec(memory_space=pltpu.VMEM)


def _ffn_kernel(*refs, merge, final):
    it = iter(refs)
    x_ref = next(it)
    if merge:
        ya_ref, yb_ref, gate_ref, wo_ref = next(it), next(it), next(it), next(it)
    ng_ref, wg_ref, wu_ref, wd_ref = next(it), next(it), next(it), next(it)
    if final:
        fg_ref = next(it)
    o_ref = next(it)
    a_scr = next(it)

    x = x_ref[...]
    if merge:
        merged = (gate_ref[:, :D_MODEL] * ya_ref[...]
                  + gate_ref[:, D_MODEL:] * yb_ref[...]).astype(BF16)
        x = x + _dot(merged, wo_ref[...])
    h = _rms_norm(x, ng_ref[...]).astype(BF16)
    for c in range(0, D_FF, FF_CHUNK):
        g = _dot(h, wg_ref[:, c:c + FF_CHUNK])
        u = _dot(h, wu_ref[:, c:c + FF_CHUNK])
        a_scr[:, c:c + FF_CHUNK] = (g * jax.nn.sigmoid(g) * u).astype(BF16)
    y = x + FFN_RES_WEIGHT * _dot(a_scr[...], wd_ref[...])
    if final:
        y = _rms_norm(y, fg_ref[...])
    o_ref[...] = y


def _ffn_call(x, ng, wg, wu, wd, *, tm, merge=None, final_g=None):
    t = x.shape[0]
    row = lambda w: pl.BlockSpec((tm, w), lambda i: (i, 0))
    args, specs = [x], [row(D_MODEL)]
    if merge is not None:
        ya, yb, gates, wo = merge
        args += [ya, yb, gates, wo]
        specs += [row(D_MODEL), row(D_MODEL), row(2 * D_MODEL), _whole()]
    args += [ng, wg, wu, wd]
    specs += [_whole()] * 4
    if final_g is not None:
        args.append(final_g)
        specs.append(_whole())
    return pl.pallas_call(
        functools.partial(_ffn_kernel, merge=merge is not None, final=final_g is not None),
        grid=(t // tm,),
        in_specs=specs,
        out_specs=row(D_MODEL),
        out_shape=jax.ShapeDtypeStruct((t, D_MODEL), F32),
        scratch_shapes=[pltpu.VMEM((tm, D_FF), BF16)],
        compiler_params=pltpu.CompilerParams(
            dimension_semantics=("arbitrary",), vmem_limit_bytes=VMEM_LIMIT),
        name="ffn_merge" if merge is not None else "ffn",
    )(*args)


def _proj_kernel(x_ref, ng_ref, wglu_ref, wz_ref, wxbc_ref, wdt_ref, wgate_ref, dtb_ref,
                 u_ref, z_ref, xbc_ref, dt_ref, gate_ref):
    h = _rms_norm(x_ref[...], ng_ref[...]).astype(BF16)
    glu = _dot(h, wglu_ref[...])
    u_ref[...] = glu[:, :D_CONV] * jax.nn.sigmoid(glu[:, D_CONV:])
    z_ref[...] = _dot(h, wz_ref[...])
    xbc_ref[...] = _dot(h, wxbc_ref[...])
    dt_ref[...] = jax.nn.softplus(_dot(h, wdt_ref[...]) + dtb_ref[...])
    gate_ref[...] = jax.nn.sigmoid(_dot(h, wgate_ref[...]))


def _proj_call(x, ng, wglu, wz, wxbc, wdt, wgate, dtb, *, tm):
    t = x.shape[0]
    row = lambda w: pl.BlockSpec((tm, w), lambda i: (i, 0))
    widths = (D_CONV, D_INNER, D_XBC, LANES, 2 * D_MODEL)
    return pl.pallas_call(
        _proj_kernel,
        grid=(t // tm,),
        in_specs=[row(D_MODEL)] + [_whole()] * 7,
        out_specs=[row(w) for w in widths],
        out_shape=[jax.ShapeDtypeStruct((t, w), F32) for w in widths],
        compiler_params=pltpu.CompilerParams(
            dimension_semantics=("arbitrary",), vmem_limit_bytes=VMEM_LIMIT),
        name="in_proj",
    )(x, ng, wglu, wz, wxbc, wdt, wgate, dtb)


CONV_ROWS = 64


def _conv_a_kernel(u_ref, st_ref, w_ref, b_ref, lng_ref, lnb_ref, wpw_ref,
                   y_ref, nst_ref, ext_ref, v_ref, *, tm):
    l = pl.program_id(1)

    @pl.when(l == 0)
    def _():
        ext_ref[0:CONV_A_PAD, :] = st_ref[...]

    ext_ref[CONV_A_PAD:CONV_A_PAD + tm, :] = u_ref[...]

    base = CONV_A_PAD - (CONV_A_WIDTH - 1)
    rows = min(CONV_ROWS, tm)
    for cs in range(0, D_CONV, LANES):
        wcol = w_ref[:, cs:cs + LANES]
        bcol = b_ref[:, cs:cs + LANES]
        for r0 in range(0, tm, rows):
            acc = jnp.broadcast_to(bcol, (rows, LANES))
            for k in range(CONV_A_WIDTH):
                acc = acc + wcol[k:k + 1, :] * ext_ref[r0 + base + k:r0 + base + k + rows,
                                                       cs:cs + LANES]
            v_ref[r0:r0 + rows, cs:cs + LANES] = acc

    nst_ref[...] = ext_ref[tm:tm + CONV_A_PAD, :]
    ext_ref[0:CONV_A_PAD, :] = ext_ref[tm:tm + CONV_A_PAD, :]

    v = v_ref[...]
    mu = jnp.mean(v, axis=-1, keepdims=True)
    vc = v - mu
    var = jnp.mean(vc * vc, axis=-1, keepdims=True)
    vn = vc * lax.rsqrt(var + LN_EPS) * lng_ref[...] + lnb_ref[...]
    y_ref[...] = _dot((vn * jax.nn.sigmoid(vn)).astype(BF16), wpw_ref[...])


def _conv_a_call(u, st, w, b, lng, lnb, wpw, *, tm):
    bsz, seq, _ = u.shape
    tile = lambda w_: pl.BlockSpec((None, tm, w_), lambda i, j: (i, j, 0))
    state = pl.BlockSpec((None, CONV_A_PAD, D_CONV), lambda i, j: (i, 0, 0))
    return pl.pallas_call(
        functools.partial(_conv_a_kernel, tm=tm),
        grid=(bsz, seq // tm),
        in_specs=[tile(D_CONV), state] + [_whole()] * 5,
        out_specs=[tile(D_MODEL), state],
        out_shape=[jax.ShapeDtypeStruct((bsz, seq, D_MODEL), F32),
                   jax.ShapeDtypeStruct((bsz, CONV_A_PAD, D_CONV), F32)],
        scratch_shapes=[pltpu.VMEM((tm + CONV_A_PAD, D_CONV), F32),
                        pltpu.VMEM((tm, D_CONV), F32)],
        compiler_params=pltpu.CompilerParams(
            dimension_semantics=("arbitrary", "arbitrary"), vmem_limit_bytes=VMEM_LIMIT),
        name="conv_a",
    )(u, st, w, b, lng, lnb, wpw)


def _cumsum_rows(a):
    q = a.shape[0]
    row = lax.broadcasted_iota(jnp.int32, a.shape, 0)
    s = 1
    while s < q:
        a = a + jnp.where(row >= s, pltpu.roll(a, s, axis=0), 0.0)
        s *= 2
    return a


def _expand_heads(v):
    q = v.shape[0]
    lo = lax.broadcasted_iota(jnp.int32, (q, LANES), 1) < HEAD_DIM
    per_vreg = LANES // HEAD_DIM
    pieces = []
    for p in range(N_HEADS // per_vreg):
        first = jnp.broadcast_to(v[:, per_vreg * p:per_vreg * p + 1], (q, LANES))
        second = jnp.broadcast_to(v[:, per_vreg * p + 1:per_vreg * p + 2], (q, LANES))
        pieces.append(jnp.where(lo, first, second))
    return jnp.concatenate(pieces, axis=1)


def _ssd_kernel(xbc_ref, z_ref, dt_ref, cst_ref, hst_ref, w_ref, b_ref, alog_ref, dskip_ref,
                ng_ref, wout_ref,
                y_ref, ncst_ref, nhst_ref,
                ext_ref, xc_ref, ht_ref, yg_ref, *, tm, rows_in, n_l):
    l = pl.program_id(1)
    q = SSD_CHUNK

    @pl.when(l == 0)
    def _():
        ext_ref[0:CONV_B_PAD, :] = cst_ref[...]
        for g in range(N_GROUPS):
            ht_ref[g] = hst_ref[g].T

    if rows_in < tm:
        ext_ref[CONV_B_PAD + rows_in:CONV_B_PAD + tm, :] = jnp.zeros((tm - rows_in, D_XBC), F32)
    ext_ref[CONV_B_PAD:CONV_B_PAD + rows_in, :] = xbc_ref[...]

    base = CONV_B_PAD - (CONV_B_WIDTH - 1)
    acc = jnp.broadcast_to(b_ref[...], (tm, D_XBC))
    for k in range(CONV_B_WIDTH):
        acc = acc + w_ref[k:k + 1, :] * ext_ref[base + k:base + k + tm, :]
    xc_ref[...] = acc * jax.nn.sigmoid(acc)

    ncst_ref[...] = ext_ref[rows_in:rows_in + CONV_B_PAD, :]
    ext_ref[0:CONV_B_PAD, :] = ext_ref[rows_in:rows_in + CONV_B_PAD, :]

    a_row = -jnp.exp(alog_ref[...])
    head_ok = lax.broadcasted_iota(jnp.int32, (1, LANES), 1) < N_HEADS
    a_row = jnp.where(head_ok, a_row, 0.0)
    causal = (lax.broadcasted_iota(jnp.int32, (q, q), 0)
              >= lax.broadcasted_iota(jnp.int32, (q, q), 1))
    lane_lo = lax.broadcasted_iota(jnp.int32, (q, LANES), 1) < HEAD_DIM

    for c0 in range(0, tm, q):
        if rows_in >= tm:
            dt = dt_ref[c0:c0 + q, :]
        else:
            dt = jnp.concatenate([dt_ref[...], jnp.zeros((q - rows_in, LANES), F32)], axis=0)
        acum = _cumsum_rows(dt * a_row)
        a_last = acum[q - 1:q, :]
        e_acum = jnp.exp(acum)
        s_tail = dt * jnp.exp(a_last - acum)
        e_last = jnp.exp(a_last)
        acum_t = acum.T
        dt_t = dt.T

        e_acum_x = _expand_heads(e_acum)
        s_tail_x = _expand_heads(s_tail)
        e_last_x = _expand_heads(e_last)

        xs = xc_ref[c0:c0 + q, 0:D_INNER]
        w_all = (xs * s_tail_x).astype(BF16)
        xs_b = xs.astype(BF16)
        zero_b = jnp.zeros((q, LANES), BF16)

        for g in range(N_GROUPS):
            b_g = xc_ref[c0:c0 + q, D_INNER + g * D_STATE:D_INNER + (g + 1) * D_STATE]
            c_g = xc_ref[c0:c0 + q, D_INNER + (N_GROUPS + g) * D_STATE:
                         D_INNER + (N_GROUPS + g + 1) * D_STATE]
            c_gb = c_g.astype(BF16)
            cb = lax.dot_general(c_gb, b_g.astype(BF16), (((1,), (1,)), ((), ())),
                                 preferred_element_type=F32)
            gs = slice(g * GROUP_WIDTH, (g + 1) * GROUP_WIDTH)
            ht = ht_ref[g]
            y_g = _dot(c_gb, ht.astype(BF16)) * e_acum_x[:, gs]
            for pr in range(HEADS_PER_GROUP // 2):
                ms = []
                for hh in range(2):
                    hd = g * HEADS_PER_GROUP + 2 * pr + hh
                    seg = acum[:, hd:hd + 1] - acum_t[hd:hd + 1, :]
                    decay = jnp.exp(jnp.where(causal, seg, NEG_BIG))
                    ms.append((cb * decay * dt_t[hd:hd + 1, :]).astype(BF16))
                lhs = jnp.concatenate(ms, axis=1)
                cols = slice(g * GROUP_WIDTH + pr * LANES, g * GROUP_WIDTH + (pr + 1) * LANES)
                xp = xs_b[:, cols]
                rhs = jnp.concatenate([jnp.where(lane_lo, xp, zero_b),
                                       jnp.where(lane_lo, zero_b, xp)], axis=0)
                yg_ref[c0:c0 + q, cols] = _dot(lhs, rhs) + y_g[:, pr * LANES:(pr + 1) * LANES]
            ht_ref[g] = e_last_x[:, gs] * ht + _dot(b_g.T.astype(BF16), w_all[:, gs])

    @pl.when(l == n_l - 1)
    def _():
        for g in range(N_GROUPS):
            nhst_ref[g] = ht_ref[g].T

    xs_all = xc_ref[0:rows_in, 0:D_INNER]
    z = z_ref[...]
    gated = (yg_ref[0:rows_in, :] + dskip_ref[...] * xs_all) * (z * jax.nn.sigmoid(z))
    parts = []
    for g in range(N_GROUPS):
        gg = gated[:, g * GROUP_WIDTH:(g + 1) * GROUP_WIDTH]
        ms = jnp.mean(gg * gg, axis=-1, keepdims=True)
        parts.append(gg * lax.rsqrt(ms + RMS_EPS))
    normed = jnp.concatenate(parts, axis=1) * ng_ref[...]
    y_ref[...] = _dot(normed.astype(BF16), wout_ref[...])


def _ssd_call(xbc, z, dt, cst, hst, w, b, alog, dskip, ng, wout, *, tm):
    bsz, seq, _ = xbc.shape
    rows_in = min(tm, seq)
    tm_k = max(tm, SSD_CHUNK) if seq < SSD_CHUNK else tm
    n_l = seq // rows_in
    tile = lambda w_: pl.BlockSpec((None, rows_in, w_), lambda i, j: (i, j, 0))
    cstate = pl.BlockSpec((None, CONV_B_PAD, D_XBC), lambda i, j: (i, 0, 0))
    hstate = pl.BlockSpec((None, N_GROUPS, GROUP_WIDTH, D_STATE), lambda i, j: (i, 0, 0, 0))
    return pl.pallas_call(
        functools.partial(_ssd_kernel, tm=tm_k, rows_in=rows_in, n_l=n_l),
        grid=(bsz, n_l),
        in_specs=[tile(D_XBC), tile(D_INNER), tile(LANES), cstate, hstate] + [_whole()] * 6,
        out_specs=[tile(D_MODEL), cstate, hstate],
        out_shape=[jax.ShapeDtypeStruct((bsz, seq, D_MODEL), F32),
                   jax.ShapeDtypeStruct((bsz, CONV_B_PAD, D_XBC), F32),
                   jax.ShapeDtypeStruct((bsz, N_GROUPS, GROUP_WIDTH, D_STATE), F32)],
        scratch_shapes=[pltpu.VMEM((tm_k + CONV_B_PAD, D_XBC), F32),
                        pltpu.VMEM((tm_k, D_XBC), F32),
                        pltpu.VMEM((N_GROUPS, D_STATE, GROUP_WIDTH), F32),
                        pltpu.VMEM((tm_k, D_INNER), F32)],
        compiler_params=pltpu.CompilerParams(
            dimension_semantics=("arbitrary", "arbitrary"), vmem_limit_bytes=VMEM_LIMIT),
        name="ssd",
    )(xbc, z, dt, cst, hst, w, b, alog, dskip, ng, wout)


def _pad_rows_front(a, rows):
    pad = rows - a.shape[-2]
    return jnp.pad(a, [(0, 0)] * (a.ndim - 2) + [(pad, 0), (0, 0)])


def _prepare_params(p):
    w_in = p["w_in"][0]
    s0, s1, s2, s3 = 2 * D_CONV, 2 * D_CONV + D_INNER, 2 * D_CONV + D_INNER + D_XBC, \
        2 * D_CONV + D_INNER + D_XBC + N_HEADS
    row = lambda v: v.reshape(1, -1).astype(F32)
    lane_pad = lambda v: jnp.pad(v, ((0, 0), (0, LANES - v.shape[1])))
    q = {}
    for name in ("ffn1", "ffn2"):
        q[name] = (row(p["norm_" + name][0]), p[name + "_w_gate"][0].astype(BF16),
                   p[name + "_w_up"][0].astype(BF16), p[name + "_w_down"][0].astype(BF16))
    q["proj"] = (row(p["norm_mix"][0]), w_in[:, :s0].astype(BF16), w_in[:, s0:s1].astype(BF16),
                 w_in[:, s1:s2].astype(BF16), lane_pad(w_in[:, s2:s3]).astype(BF16),
                 w_in[:, s3:].astype(BF16), lane_pad(row(p["dt_bias"][0])))
    q["conv_a"] = (p["conv_a_w"][0], row(p["conv_a_b"][0]),
                   row(p["ln_conv_g"][0]), row(p["ln_conv_b"][0]),
                   p["w_conv_a_out"][0].astype(BF16))
    q["ssd"] = (p["conv_b_w"][0], row(p["conv_b_b"][0]), lane_pad(row(p["a_log"][0])),
                row(jnp.repeat(p["d_skip"][0], HEAD_DIM)), row(p["ssm_norm_g"][0]),
                p["w_ssm_out"][0].astype(BF16))
    q["w_out"] = p["w_out"][0].astype(BF16)
    q["norm_final"] = row(p["norm_final"])
    return q


def _encoder(x, st_a, st_b, st_h, q, *, tm_ffn, tm_mix):
    bsz, seq, _ = x.shape
    t = bsz * seq
    x0 = x.reshape(t, D_MODEL)
    x1 = _ffn_call(x0, *q["ffn1"], tm=min(tm_ffn, t))
    u, z, xbc, dt, gates = _proj_call(x1, *q["proj"], tm=min(tm_mix, t))
    tm_seq = min(tm_mix, seq)
    shp = lambda a: a.reshape(bsz, seq, a.shape[-1])
    y_a, new_a = _conv_a_call(shp(u), _pad_rows_front(st_a, CONV_A_PAD), *q["conv_a"], tm=tm_seq)
    y_b, new_b, new_h = _ssd_call(
        shp(xbc), shp(z), shp(dt), _pad_rows_front(st_b, CONV_B_PAD),
        st_h.reshape(bsz, N_GROUPS, GROUP_WIDTH, D_STATE), *q["ssd"], tm=tm_seq)
    y = _ffn_call(x1, *q["ffn2"], tm=min(tm_mix, t),
                  merge=(y_a.reshape(t, D_MODEL), y_b.reshape(t, D_MODEL), gates, q["w_out"]),
                  final_g=q["norm_final"])
    return (y.reshape(bsz, seq, D_MODEL),
            new_a[None, :, CONV_A_PAD - (CONV_A_WIDTH - 1):],
            new_b[None, :, CONV_B_PAD - (CONV_B_WIDTH - 1):],
            new_h.reshape(1, bsz, N_HEADS, HEAD_DIM, D_STATE))


def kernel(x_prompt, x_sample, state_conv_a, state_conv_b, state_ssm, norm_ffn1, ffn1_w_gate, ffn1_w_up, ffn1_w_down, norm_mix, w_in, conv_a_w, conv_a_b, ln_conv_g, ln_conv_b, w_conv_a_out, conv_b_w, conv_b_b, dt_bias, a_log, d_skip, ssm_norm_g, w_ssm_out, w_out, norm_ffn2, ffn2_w_gate, ffn2_w_up, ffn2_w_down, norm_final):
    p = dict(norm_ffn1=norm_ffn1, ffn1_w_gate=ffn1_w_gate, ffn1_w_up=ffn1_w_up,
             ffn1_w_down=ffn1_w_down, norm_mix=norm_mix, w_in=w_in, conv_a_w=conv_a_w,
             conv_a_b=conv_a_b, ln_conv_g=ln_conv_g, ln_conv_b=ln_conv_b,
             w_conv_a_out=w_conv_a_out, conv_b_w=conv_b_w, conv_b_b=conv_b_b, dt_bias=dt_bias,
             a_log=a_log, d_skip=d_skip, ssm_norm_g=ssm_norm_g, w_ssm_out=w_ssm_out, w_out=w_out,
             norm_ffn2=norm_ffn2, ffn2_w_gate=ffn2_w_gate, ffn2_w_up=ffn2_w_up,
             ffn2_w_down=ffn2_w_down, norm_final=norm_final)
    q = _prepare_params(p)
    bp = x_prompt.shape[0]
    zero_a = jnp.zeros((bp, CONV_A_WIDTH - 1, D_CONV), F32)
    zero_b = jnp.zeros((bp, CONV_B_WIDTH - 1, D_XBC), F32)
    zero_h = jnp.zeros((bp, N_HEADS, HEAD_DIM, D_STATE), F32)
    y_p, a_p, b_p, h_p = _encoder(x_prompt, zero_a, zero_b, zero_h, q, tm_ffn=512, tm_mix=256)
    y_s, a_s, b_s, h_s = _encoder(x_sample, state_conv_a[0], state_conv_b[0], state_ssm[0], q,
                                  tm_ffn=256, tm_mix=256)
    return (y_p, y_s, a_p, b_p, h_p, a_s, b_s, h_s)
```

```python
import functools

import jax
import jax.numpy as jnp
import numpy as np
from jax import lax
from jax.experimental import pallas as pl
from jax.experimental.pallas import tpu as pltpu

D_MODEL = 1024
D_FF = 2816
D_CONV = D_MODEL
CONV_A_WIDTH = 31
D_INNER = 2048
HEAD_DIM = 64
N_HEADS = D_INNER // HEAD_DIM
N_GROUPS = 4
HEADS_PER_GROUP = N_HEADS // N_GROUPS
GROUP_WIDTH = D_INNER // N_GROUPS
D_STATE = 128
CONV_B_WIDTH = 4
D_XBC = D_INNER + 2 * N_GROUPS * D_STATE
RMS_EPS = 1e-6
LN_EPS = 1e-5
FFN_RES_WEIGHT = 0.5

LANES = 128
SUBLANES = 8
CONV_A_PAD = 32
CONV_B_PAD = 8
SSD_CHUNK = 128
HEAD_COPIES = 3
FF_CHUNK = 256
VMEM_LIMIT = 56 * 1024 * 1024
NEG_BIG = -1e30

F32 = jnp.float32
BF16 = jnp.bfloat16


def _dot(a, b):
    return jnp.dot(a, b, preferred_element_type=F32)


def _rms_norm(x, g):
    ms = jnp.mean(x * x, axis=-1, keepdims=True)
    return x * lax.rsqrt(ms + RMS_EPS) * g


def _whole():
    return pl.BlockSpec(memory_space=pltpu.VMEM)


def _ffn_kernel(*refs, merge, final):
    it = iter(refs)
    x_ref = next(it)
    if merge:
        ya_ref, yb_ref, gate_ref, wo_ref = next(it), next(it), next(it), next(it)
    ng_ref, wg_ref, wu_ref, wd_ref = next(it), next(it), next(it), next(it)
    if final:
        fg_ref = next(it)
    o_ref = next(it)
    a_scr = next(it)

    x = x_ref[...]
    if merge:
        merged = (gate_ref[:, :D_MODEL] * ya_ref[...]
                  + gate_ref[:, D_MODEL:] * yb_ref[...]).astype(BF16)
        x = x + _dot(merged, wo_ref[...])
    h = _rms_norm(x, ng_ref[...]).astype(BF16)
    for c in range(0, D_FF, FF_CHUNK):
        g = _dot(h, wg_ref[:, c:c + FF_CHUNK])
        u = _dot(h, wu_ref[:, c:c + FF_CHUNK])
        a_scr[:, c:c + FF_CHUNK] = (g * jax.nn.sigmoid(g) * u).astype(BF16)
    y = x + FFN_RES_WEIGHT * _dot(a_scr[...], wd_ref[...])
    if final:
        y = _rms_norm(y, fg_ref[...])
    o_ref[...] = y


def _ffn_call(x, ng, wg, wu, wd, *, tm, merge=None, final_g=None):
    t = x.shape[0]
    row = lambda w: pl.BlockSpec((tm, w), lambda i: (i, 0))
    args, specs = [x], [row(D_MODEL)]
    if merge is not None:
        ya, yb, gates, wo = merge
        args += [ya, yb, gates, wo]
        specs += [row(D_MODEL), row(D_MODEL), row(2 * D_MODEL), _whole()]
    args += [ng, wg, wu, wd]
    specs += [_whole()] * 4
    if final_g is not None:
        args.append(final_g)
        specs.append(_whole())
    return pl.pallas_call(
        functools.partial(_ffn_kernel, merge=merge is not None, final=final_g is not None),
        grid=(t // tm,),
        in_specs=specs,
        out_specs=row(D_MODEL),
        out_shape=jax.ShapeDtypeStruct((t, D_MODEL), F32),
        scratch_shapes=[pltpu.VMEM((tm, D_FF), BF16)],
        compiler_params=pltpu.CompilerParams(
            dimension_semantics=("arbitrary",), vmem_limit_bytes=VMEM_LIMIT),
        name="ffn_merge" if merge is not None else "ffn",
    )(*args)


def _proj_kernel(x_ref, ng_ref, wglu_ref, wz_ref, wxbc_ref, wdt_ref, wgate_ref, dtb_ref,
                 u_ref, z_ref, xbc_ref, dt_ref, gate_ref):
    h = _rms_norm(x_ref[...], ng_ref[...]).astype(BF16)
    glu = _dot(h, wglu_ref[...])
    u_ref[...] = glu[:, :D_CONV] * jax.nn.sigmoid(glu[:, D_CONV:])
    z_ref[...] = _dot(h, wz_ref[...])
    xbc_ref[...] = _dot(h, wxbc_ref[...])
    dt_ref[...] = jax.nn.softplus(_dot(h, wdt_ref[...]) + dtb_ref[...])
    gate_ref[...] = jax.nn.sigmoid(_dot(h, wgate_ref[...]))


def _proj_call(x, ng, wglu, wz, wxbc, wdt, wgate, dtb, *, tm):
    t = x.shape[0]
    row = lambda w: pl.BlockSpec((tm, w), lambda i: (i, 0))
    widths = (D_CONV, D_INNER, D_XBC, LANES, 2 * D_MODEL)
    return pl.pallas_call(
        _proj_kernel,
        grid=(t // tm,),
        in_specs=[row(D_MODEL)] + [_whole()] * 7,
        out_specs=[row(w) for w in widths],
        out_shape=[jax.ShapeDtypeStruct((t, w), F32) for w in widths],
        compiler_params=pltpu.CompilerParams(
            dimension_semantics=("arbitrary",), vmem_limit_bytes=VMEM_LIMIT),
        name="in_proj",
    )(x, ng, wglu, wz, wxbc, wdt, wgate, dtb)


def _store_time_major(dst_ref, t0, src, n, nb):
    for j in range(nb):
        dst_ref[pl.ds(t0 * nb + j, n, stride=nb), :] = src[:, j * LANES:(j + 1) * LANES]


def _load_time_major(dst_ref, src_ref, t0, n, nb):
    for j in range(nb):
        dst_ref[:, j * LANES:(j + 1) * LANES] = src_ref[pl.ds(t0 * nb + j, n, stride=nb), :]


def _conv_time_major(ext_ref, w_ref, b_ref, out_ref, *, n, nb, taps, lag0, steps, act=None):
    steps = min(steps, n)
    rep = lambda tile: jnp.concatenate([tile] * steps, axis=0)
    for t0 in range(0, n, steps):
        acc = rep(b_ref[...])
        for k in range(taps):
            lo = (t0 + lag0 + k) * nb
            acc = acc + rep(w_ref[k * nb:(k + 1) * nb, :]) * ext_ref[lo:lo + steps * nb, :]
        out_ref[t0 * nb:(t0 + steps) * nb, :] = acc if act is None else act(acc)


def _silu(x):
    return x * jax.nn.sigmoid(x)


def _conv_a_kernel(u_ref, st_ref, w_ref, b_ref, lng_ref, lnb_ref, wpw_ref,
                   y_ref, nst_ref, ext_ref, out_ref, v_ref, *, tm):
    l = pl.program_id(1)
    nb = D_CONV // LANES

    @pl.when(l == 0)
    def _():
        _store_time_major(ext_ref, 0, st_ref, CONV_A_PAD, nb)

    _store_time_major(ext_ref, CONV_A_PAD, u_ref, tm, nb)
    _conv_time_major(ext_ref, w_ref, b_ref, out_ref, n=tm, nb=nb, taps=CONV_A_WIDTH,
                     lag0=CONV_A_PAD - (CONV_A_WIDTH - 1), steps=16)
    _load_time_major(nst_ref, ext_ref, tm, CONV_A_PAD, nb)
    ext_ref[0:CONV_A_PAD * nb, :] = ext_ref[tm * nb:(tm + CONV_A_PAD) * nb, :]
    _load_time_major(v_ref, out_ref, 0, tm, nb)

    v = v_ref[...]
    mu = jnp.mean(v, axis=-1, keepdims=True)
    vc = v - mu
    var = jnp.mean(vc * vc, axis=-1, keepdims=True)
    vn = vc * lax.rsqrt(var + LN_EPS) * lng_ref[...] + lnb_ref[...]
    y_ref[...] = _dot((vn * jax.nn.sigmoid(vn)).astype(BF16), wpw_ref[...])


def _conv_a_call(u, st, w, b, lng, lnb, wpw, *, tm):
    bsz, seq, _ = u.shape
    tile = lambda w_: pl.BlockSpec((None, tm, w_), lambda i, j: (i, j, 0))
    state = pl.BlockSpec((None, CONV_A_PAD, D_CONV), lambda i, j: (i, 0, 0))
    return pl.pallas_call(
        functools.partial(_conv_a_kernel, tm=tm),
        grid=(bsz, seq // tm),
        in_specs=[tile(D_CONV), state] + [_whole()] * 5,
        out_specs=[tile(D_MODEL), state],
        out_shape=[jax.ShapeDtypeStruct((bsz, seq, D_MODEL), F32),
                   jax.ShapeDtypeStruct((bsz, CONV_A_PAD, D_CONV), F32)],
        scratch_shapes=[pltpu.VMEM(((tm + CONV_A_PAD) * (D_CONV // LANES), LANES), F32),
                        pltpu.VMEM((tm * (D_CONV // LANES), LANES), F32),
                        pltpu.VMEM((tm, D_CONV), F32)],
        compiler_params=pltpu.CompilerParams(
            dimension_semantics=("arbitrary", "arbitrary"), vmem_limit_bytes=VMEM_LIMIT),
        name="conv_a",
    )(u, st, w, b, lng, lnb, wpw)


def _layer_norm_swish(v, g, b):
    mu = jnp.mean(v, axis=-1, keepdims=True)
    vc = v - mu
    var = jnp.mean(vc * vc, axis=-1, keepdims=True)
    return _silu(vc * lax.rsqrt(var + LN_EPS) * g + b)


def _proj_conv_kernel(x_ref, st_ref, ng_ref, wglu_ref, wz_ref, wxbc_ref, wdt_ref, wgate_ref,
                      dtb_ref, cw_ref, cb_ref, lng_ref, lnb_ref, wpw_ref,
                      ya_ref, nst_ref, z_ref, xbc_ref, dt_ref, gate_ref,
                      ext_ref, out_ref, v_ref, *, tm):
    l = pl.program_id(1)
    nb = D_CONV // LANES

    @pl.when(l == 0)
    def _():
        _store_time_major(ext_ref, 0, st_ref, CONV_A_PAD, nb)

    h = _rms_norm(x_ref[...], ng_ref[...]).astype(BF16)
    glu = _dot(h, wglu_ref[...])
    _store_time_major(ext_ref, CONV_A_PAD, glu[:, :D_CONV] * jax.nn.sigmoid(glu[:, D_CONV:]),
                      tm, nb)
    z_ref[...] = _dot(h, wz_ref[...])
    xbc_ref[...] = _dot(h, wxbc_ref[...])
    dt_ref[...] = jax.nn.softplus(_dot(h, wdt_ref[...]) + dtb_ref[...])
    gate_ref[...] = jax.nn.sigmoid(_dot(h, wgate_ref[...]))
    _conv_time_major(ext_ref, cw_ref, cb_ref, out_ref, n=tm, nb=nb, taps=CONV_A_WIDTH,
                     lag0=CONV_A_PAD - (CONV_A_WIDTH - 1), steps=16)
    _load_time_major(nst_ref, ext_ref, tm, CONV_A_PAD, nb)
    ext_ref[0:CONV_A_PAD * nb, :] = ext_ref[tm * nb:(tm + CONV_A_PAD) * nb, :]
    _load_time_major(v_ref, out_ref, 0, tm, nb)
    ya_ref[...] = _dot(_layer_norm_swish(v_ref[...], lng_ref[...], lnb_ref[...]).astype(BF16),
                       wpw_ref[...])


def _proj_conv_call(x, st, proj_w, conv_w, *, tm):
    bsz, seq, _ = x.shape
    nb = D_CONV // LANES
    tile = lambda w_: pl.BlockSpec((None, tm, w_), lambda i, j: (i, j, 0))
    state = pl.BlockSpec((None, CONV_A_PAD, D_CONV), lambda i, j: (i, 0, 0))
    widths = (D_MODEL, D_INNER, D_XBC, LANES, 2 * D_MODEL)
    outs = [jax.ShapeDtypeStruct((bsz, seq, w), F32) for w in widths]
    outs.insert(1, jax.ShapeDtypeStruct((bsz, CONV_A_PAD, D_CONV), F32))
    out_specs = [tile(w) for w in widths]
    out_specs.insert(1, state)
    return pl.pallas_call(
        functools.partial(_proj_conv_kernel, tm=tm),
        grid=(bsz, seq // tm),
        in_specs=[tile(D_MODEL), state] + [_whole()] * (len(proj_w) + len(conv_w)),
        out_specs=out_specs,
        out_shape=outs,
        scratch_shapes=[pltpu.VMEM(((tm + CONV_A_PAD) * nb, LANES), F32),
                        pltpu.VMEM((tm * nb, LANES), F32),
                        pltpu.VMEM((tm, D_CONV), F32)],
        compiler_params=pltpu.CompilerParams(
            dimension_semantics=("arbitrary", "arbitrary"), vmem_limit_bytes=VMEM_LIMIT),
        name="proj_conv",
    )(x, st, *proj_w, *conv_w)


def _cumsum_rows(a):
    q = a.shape[0]
    row = lax.broadcasted_iota(jnp.int32, a.shape, 0)
    s = 1
    while s < q:
        a = a + jnp.where(row >= s, pltpu.roll(a, s, axis=0), 0.0)
        s *= 2
    return a


def _expand_heads(v, e_ref):
    lane = lax.broadcasted_iota(jnp.int32, v.shape, 1)
    hi = v.astype(BF16)
    rest = v - hi.astype(F32)
    mid = rest.astype(BF16)
    lo = (rest - mid.astype(F32)).astype(BF16)
    terms = jnp.where(lane < N_HEADS, hi, jnp.where(lane < 2 * N_HEADS, mid, lo))
    return _dot(terms, e_ref[...])


def _ssd_kernel(xbc_ref, z_ref, dt_ref, cst_ref, hst_ref, w_ref, b_ref, alog_ref, dskip_ref,
                ng_ref, wout_ref, expand_ref,
                y_ref, ncst_ref, nhst_ref,
                ext_ref, cout_ref, xc_ref, ht_ref, yg_ref, *, tm, rows_in, n_l):
    l = pl.program_id(1)
    q = SSD_CHUNK
    nb = D_XBC // LANES

    @pl.when(l == 0)
    def _():
        _store_time_major(ext_ref, 0, cst_ref, CONV_B_PAD, nb)
        for g in range(N_GROUPS):
            ht_ref[g] = hst_ref[g].T

    if rows_in < tm:
        ext_ref[(CONV_B_PAD + rows_in) * nb:(CONV_B_PAD + tm) * nb, :] = jnp.zeros(
            ((tm - rows_in) * nb, LANES), F32)
    _store_time_major(ext_ref, CONV_B_PAD, xbc_ref, rows_in, nb)
    _conv_time_major(ext_ref, w_ref, b_ref, cout_ref, n=tm, nb=nb, taps=CONV_B_WIDTH,
                     lag0=CONV_B_PAD - (CONV_B_WIDTH - 1), steps=8, act=_silu)
    _load_time_major(ncst_ref, ext_ref, rows_in, CONV_B_PAD, nb)
    ext_ref[0:CONV_B_PAD * nb, :] = ext_ref[rows_in * nb:(rows_in + CONV_B_PAD) * nb, :]
    _load_time_major(xc_ref, cout_ref, 0, tm, nb)

    a_row = -jnp.exp(alog_ref[...])
    head_ok = lax.broadcasted_iota(jnp.int32, (1, LANES), 1) < HEAD_COPIES * N_HEADS
    a_row = jnp.where(head_ok, a_row, 0.0)
    causal = (lax.broadcasted_iota(jnp.int32, (q, q), 0)
              >= lax.broadcasted_iota(jnp.int32, (q, q), 1))
    lane_lo = lax.broadcasted_iota(jnp.int32, (q, LANES), 1) < HEAD_DIM

    for c0 in range(0, tm, q):
        if rows_in >= tm:
            dt = dt_ref[c0:c0 + q, :]
        else:
            dt = jnp.concatenate([dt_ref[...], jnp.zeros((q - rows_in, LANES), F32)], axis=0)
        acum = _cumsum_rows(dt * a_row)
        a_last = acum[q - 1:q, :]
        e_acum = jnp.exp(acum)
        s_tail = dt * jnp.exp(a_last - acum)
        acum_t = acum.T
        dt_t = dt.T

        expanded = _expand_heads(jnp.concatenate([e_acum, s_tail], axis=0), expand_ref)
        e_acum_x = expanded[0:q]
        s_tail_x = expanded[q:2 * q]
        e_last_x = expanded[q - 1:q]

        xs = xc_ref[c0:c0 + q, 0:D_INNER]
        w_all = (xs * s_tail_x).astype(BF16)
        xs_b = xs.astype(BF16)
        zero_b = jnp.zeros((q, LANES), BF16)

        for g in range(N_GROUPS):
            b_g = xc_ref[c0:c0 + q, D_INNER + g * D_STATE:D_INNER + (g + 1) * D_STATE]
            c_g = xc_ref[c0:c0 + q, D_INNER + (N_GROUPS + g) * D_STATE:
                         D_INNER + (N_GROUPS + g + 1) * D_STATE]
            c_gb = c_g.astype(BF16)
            cb = lax.dot_general(c_gb, b_g.astype(BF16), (((1,), (1,)), ((), ())),
                                 preferred_element_type=F32)
            gs = slice(g * GROUP_WIDTH, (g + 1) * GROUP_WIDTH)
            ht = ht_ref[g]
            y_g = _dot(c_gb, ht.astype(BF16)) * e_acum_x[:, gs]
            for pr in range(HEADS_PER_GROUP // 2):
                ms = []
                for hh in range(2):
                    hd = g * HEADS_PER_GROUP + 2 * pr + hh
                    seg = acum[:, hd:hd + 1] - acum_t[hd:hd + 1, :]
                    decay = jnp.exp(jnp.where(causal, seg, NEG_BIG))
                    ms.append((cb * decay * dt_t[hd:hd + 1, :]).astype(BF16))
                lhs = jnp.concatenate(ms, axis=1)
                cols = slice(g * GROUP_WIDTH + pr * LANES, g * GROUP_WIDTH + (pr + 1) * LANES)
                xp = xs_b[:, cols]
                rhs = jnp.concatenate([jnp.where(lane_lo, xp, zero_b),
                                       jnp.where(lane_lo, zero_b, xp)], axis=0)
                yg_ref[c0:c0 + q, cols] = _dot(lhs, rhs) + y_g[:, pr * LANES:(pr + 1) * LANES]
            ht_ref[g] = e_last_x[:, gs] * ht + _dot(b_g.T.astype(BF16), w_all[:, gs])

    @pl.when(l == n_l - 1)
    def _():
        for g in range(N_GROUPS):
            nhst_ref[g] = ht_ref[g].T

    xs_all = xc_ref[0:rows_in, 0:D_INNER]
    z = z_ref[...]
    gated = (yg_ref[0:rows_in, :] + dskip_ref[...] * xs_all) * (z * jax.nn.sigmoid(z))
    parts = []
    for g in range(N_GROUPS):
        gg = gated[:, g * GROUP_WIDTH:(g + 1) * GROUP_WIDTH]
        ms = jnp.mean(gg * gg, axis=-1, keepdims=True)
        parts.append(gg * lax.rsqrt(ms + RMS_EPS))
    normed = jnp.concatenate(parts, axis=1) * ng_ref[...]
    y_ref[...] = _dot(normed.astype(BF16), wout_ref[...])


def _ssd_call(xbc, z, dt, cst, hst, w, b, alog, dskip, ng, wout, expand, *, tm):
    bsz, seq, _ = xbc.shape
    rows_in = min(tm, seq)
    tm_k = max(tm, SSD_CHUNK) if seq < SSD_CHUNK else tm
    n_l = seq // rows_in
    tile = lambda w_: pl.BlockSpec((None, rows_in, w_), lambda i, j: (i, j, 0))
    cstate = pl.BlockSpec((None, CONV_B_PAD, D_XBC), lambda i, j: (i, 0, 0))
    hstate = pl.BlockSpec((None, N_GROUPS, GROUP_WIDTH, D_STATE), lambda i, j: (i, 0, 0, 0))
    return pl.pallas_call(
        functools.partial(_ssd_kernel, tm=tm_k, rows_in=rows_in, n_l=n_l),
        grid=(bsz, n_l),
        in_specs=[tile(D_XBC), tile(D_INNER), tile(LANES), cstate, hstate] + [_whole()] * 7,
        out_specs=[tile(D_MODEL), cstate, hstate],
        out_shape=[jax.ShapeDtypeStruct((bsz, seq, D_MODEL), F32),
                   jax.ShapeDtypeStruct((bsz, CONV_B_PAD, D_XBC), F32),
                   jax.ShapeDtypeStruct((bsz, N_GROUPS, GROUP_WIDTH, D_STATE), F32)],
        scratch_shapes=[pltpu.VMEM(((tm_k + CONV_B_PAD) * (D_XBC // LANES), LANES), F32),
                        pltpu.VMEM((tm_k * (D_XBC // LANES), LANES), F32),
                        pltpu.VMEM((tm_k, D_XBC), F32),
                        pltpu.VMEM((N_GROUPS, D_STATE, GROUP_WIDTH), F32),
                        pltpu.VMEM((tm_k, D_INNER), F32)],
        compiler_params=pltpu.CompilerParams(
            dimension_semantics=("arbitrary", "arbitrary"), vmem_limit_bytes=VMEM_LIMIT),
        name="ssd",
    )(xbc, z, dt, cst, hst, w, b, alog, dskip, ng, wout, expand)


def _pad_rows_front(a, rows):
    pad = rows - a.shape[-2]
    return jnp.pad(a, [(0, 0)] * (a.ndim - 2) + [(pad, 0), (0, 0)])


def _prepare_params(p):
    w_in = p["w_in"][0]
    s0, s1, s2, s3 = 2 * D_CONV, 2 * D_CONV + D_INNER, 2 * D_CONV + D_INNER + D_XBC, \
        2 * D_CONV + D_INNER + D_XBC + N_HEADS
    row = lambda v: v.reshape(1, -1).astype(F32)
    lane_pad = lambda v: jnp.pad(jnp.tile(v, (1, HEAD_COPIES)),
                                 ((0, 0), (0, LANES - HEAD_COPIES * v.shape[1])))
    lane_group = np.arange(LANES)[:, None]
    expand = ((lane_group % N_HEADS == np.arange(D_INNER)[None, :] // HEAD_DIM)
              & (lane_group < HEAD_COPIES * N_HEADS))
    q = {}
    for name in ("ffn1", "ffn2"):
        q[name] = (row(p["norm_" + name][0]), p[name + "_w_gate"][0].astype(BF16),
                   p[name + "_w_up"][0].astype(BF16), p[name + "_w_down"][0].astype(BF16))
    q["proj"] = (row(p["norm_mix"][0]), w_in[:, :s0].astype(BF16), w_in[:, s0:s1].astype(BF16),
                 w_in[:, s1:s2].astype(BF16), lane_pad(w_in[:, s2:s3]).astype(BF16),
                 w_in[:, s3:].astype(BF16), lane_pad(row(p["dt_bias"][0])))
    q["conv_a"] = (p["conv_a_w"][0].reshape(-1, LANES), p["conv_a_b"][0].reshape(-1, LANES),
                   row(p["ln_conv_g"][0]), row(p["ln_conv_b"][0]),
                   p["w_conv_a_out"][0].astype(BF16))
    q["ssd"] = (p["conv_b_w"][0].reshape(-1, LANES), p["conv_b_b"][0].reshape(-1, LANES),
                lane_pad(row(p["a_log"][0])),
                row(jnp.repeat(p["d_skip"][0], HEAD_DIM)), row(p["ssm_norm_g"][0]),
                p["w_ssm_out"][0].astype(BF16), jnp.asarray(expand, BF16))
    q["w_out"] = p["w_out"][0].astype(BF16)
    q["norm_final"] = row(p["norm_final"])
    return q


def _encoder(x, st_a, st_b, st_h, q, *, tm_ffn, tm_mix):
    bsz, seq, _ = x.shape
    t = bsz * seq
    x0 = x.reshape(t, D_MODEL)
    x1 = _ffn_call(x0, *q["ffn1"], tm=min(tm_ffn, t))
    tm_seq = min(tm_mix, seq)
    shp = lambda a: a.reshape(bsz, seq, a.shape[-1])
    st_a = _pad_rows_front(st_a, CONV_A_PAD)
    if seq >= tm_mix:
        y_a, new_a, z, xbc, dt, gates = _proj_conv_call(shp(x1), st_a, q["proj"], q["conv_a"],
                                                        tm=tm_seq)
        gates = gates.reshape(t, 2 * D_MODEL)
    else:
        u, z, xbc, dt, gates = _proj_call(x1, *q["proj"], tm=min(tm_mix, t))
        y_a, new_a = _conv_a_call(shp(u), st_a, *q["conv_a"], tm=tm_seq)
    y_b, new_b, new_h = _ssd_call(
        shp(xbc), shp(z), shp(dt), _pad_rows_front(st_b, CONV_B_PAD),
        st_h.reshape(bsz, N_GROUPS, GROUP_WIDTH, D_STATE), *q["ssd"], tm=tm_seq)
    y = _ffn_call(x1, *q["ffn2"], tm=min(tm_mix, t),
                  merge=(y_a.reshape(t, D_MODEL), y_b.reshape(t, D_MODEL), gates, q["w_out"]),
                  final_g=q["norm_final"])
    return (y.reshape(bsz, seq, D_MODEL),
            new_a[None, :, CONV_A_PAD - (CONV_A_WIDTH - 1):],
            new_b[None, :, CONV_B_PAD - (CONV_B_WIDTH - 1):],
            new_h.reshape(1, bsz, N_HEADS, HEAD_DIM, D_STATE))


def kernel(x_prompt, x_sample, state_conv_a, state_conv_b, state_ssm, norm_ffn1, ffn1_w_gate, ffn1_w_up, ffn1_w_down, norm_mix, w_in, conv_a_w, conv_a_b, ln_conv_g, ln_conv_b, w_conv_a_out, conv_b_w, conv_b_b, dt_bias, a_log, d_skip, ssm_norm_g, w_ssm_out, w_out, norm_ffn2, ffn2_w_gate, ffn2_w_up, ffn2_w_down, norm_final):
    p = dict(norm_ffn1=norm_ffn1, ffn1_w_gate=ffn1_w_gate, ffn1_w_up=ffn1_w_up,
             ffn1_w_down=ffn1_w_down, norm_mix=norm_mix, w_in=w_in, conv_a_w=conv_a_w,
             conv_a_b=conv_a_b, ln_conv_g=ln_conv_g, ln_conv_b=ln_conv_b,
             w_conv_a_out=w_conv_a_out, conv_b_w=conv_b_w, conv_b_b=conv_b_b, dt_bias=dt_bias,
             a_log=a_log, d_skip=d_skip, ssm_norm_g=ssm_norm_g, w_ssm_out=w_ssm_out, w_out=w_out,
             norm_ffn2=norm_ffn2, ffn2_w_gate=ffn2_w_gate, ffn2_w_up=ffn2_w_up,
             ffn2_w_down=ffn2_w_down, norm_final=norm_final)
    q = _prepare_params(p)
    bp = x_prompt.shape[0]
    zero_a = jnp.zeros((bp, CONV_A_WIDTH - 1, D_CONV), F32)
    zero_b = jnp.zeros((bp, CONV_B_WIDTH - 1, D_XBC), F32)
    zero_h = jnp.zeros((bp, N_HEADS, HEAD_DIM, D_STATE), F32)
    y_p, a_p, b_p, h_p = _encoder(x_prompt, zero_a, zero_b, zero_h, q, tm_ffn=512, tm_mix=256)
    y_s, a_s, b_s, h_s = _encoder(x_sample, state_conv_a[0], state_conv_b[0], state_ssm[0], q,
                                  tm_ffn=256, tm_mix=256)
    return (y_p, y_s, a_p, b_p, h_p, a_s, b_s, h_s)
```

```python
import functools

import jax
import jax.numpy as jnp
import numpy as np
from jax import lax
from jax.experimental import pallas as pl
from jax.experimental.pallas import tpu as pltpu

D_MODEL = 1024
D_FF = 2816
D_CONV = D_MODEL
CONV_A_WIDTH = 31
D_INNER = 2048
HEAD_DIM = 64
N_HEADS = D_INNER // HEAD_DIM
N_GROUPS = 4
HEADS_PER_GROUP = N_HEADS // N_GROUPS
GROUP_WIDTH = D_INNER // N_GROUPS
D_STATE = 128
CONV_B_WIDTH = 4
D_XBC = D_INNER + 2 * N_GROUPS * D_STATE
RMS_EPS = 1e-6
LN_EPS = 1e-5
FFN_RES_WEIGHT = 0.5

LANES = 128
SUBLANES = 8
CONV_A_PAD = 32
CONV_B_PAD = 8
SSD_CHUNK = 128
HEAD_COPIES = 3
FF_CHUNK = 256
PROJ_PANEL = 256
CONV_A_STEPS = 16
VMEM_LIMIT = 56 * 1024 * 1024
NEG_BIG = -1e30
LOG2_E = 1.4426950408889634

F32 = jnp.float32
BF16 = jnp.bfloat16


def _dot(a, b):
    return jnp.dot(a, b, preferred_element_type=F32)


def _rms_norm(x, g):
    ms = jnp.mean(x * x, axis=-1, keepdims=True)
    return x * lax.rsqrt(ms + RMS_EPS) * g


def _whole():
    return pl.BlockSpec(memory_space=pltpu.VMEM)


def _ffn_kernel(*refs, merge, final):
    it = iter(refs)
    x_ref = next(it)
    if merge:
        ya_ref, yb_ref, gate_ref, wo_ref = next(it), next(it), next(it), next(it)
    ng_ref, wg_ref, wu_ref, wd_ref = next(it), next(it), next(it), next(it)
    if final:
        fg_ref = next(it)
    o_ref = next(it)
    a_scr = next(it)

    x = x_ref[...]
    if merge:
        merged = (gate_ref[:, :D_MODEL] * ya_ref[...]
                  + gate_ref[:, D_MODEL:] * yb_ref[...]).astype(BF16)
        x = x + _dot(merged, wo_ref[...])
    h = _rms_norm(x, ng_ref[...]).astype(BF16)
    for c in range(0, D_FF, FF_CHUNK):
        g = _dot(h, wg_ref[:, c:c + FF_CHUNK])
        u = _dot(h, wu_ref[:, c:c + FF_CHUNK])
        a_scr[:, c:c + FF_CHUNK] = (g * jax.nn.sigmoid(g) * u).astype(BF16)
    y = x + FFN_RES_WEIGHT * _dot(a_scr[...], wd_ref[...])
    if final:
        y = _rms_norm(y, fg_ref[...])
    o_ref[...] = y


def _ffn_call(x, ng, wg, wu, wd, *, tm, merge=None, final_g=None):
    t = x.shape[0]
    row = lambda w: pl.BlockSpec((tm, w), lambda i: (i, 0))
    args, specs = [x], [row(D_MODEL)]
    if merge is not None:
        ya, yb, gates, wo = merge
        args += [ya, yb, gates, wo]
        specs += [row(D_MODEL), row(D_MODEL), row(2 * D_MODEL), _whole()]
    args += [ng, wg, wu, wd]
    specs += [_whole()] * 4
    if final_g is not None:
        args.append(final_g)
        specs.append(_whole())
    return pl.pallas_call(
        functools.partial(_ffn_kernel, merge=merge is not None, final=final_g is not None),
        grid=(t // tm,),
        in_specs=specs,
        out_specs=row(D_MODEL),
        out_shape=jax.ShapeDtypeStruct((t, D_MODEL), F32),
        scratch_shapes=[pltpu.VMEM((tm, D_FF), BF16)],
        compiler_params=pltpu.CompilerParams(
            dimension_semantics=("arbitrary",), vmem_limit_bytes=VMEM_LIMIT),
        name="ffn_merge" if merge is not None else "ffn",
    )(*args)


PROJ_GLU = (0, 2 * D_CONV)
PROJ_Z = (PROJ_GLU[1], PROJ_GLU[1] + D_INNER)
PROJ_XBC = (PROJ_Z[1], PROJ_Z[1] + D_XBC)
PROJ_GATE = (PROJ_XBC[1], PROJ_XBC[1] + 2 * D_MODEL)
PROJ_DT = (PROJ_GATE[1], PROJ_GATE[1] + LANES)


def _project(h, w_ref, seg):
    return _dot(h, w_ref[:, seg[0]:seg[1]])


def _proj_kernel(x_ref, ng_ref, w_ref, dtb_ref, u_ref, z_ref, xbc_ref, dt_ref, gate_ref):
    h = _rms_norm(x_ref[...], ng_ref[...]).astype(BF16)
    glu = _project(h, w_ref, PROJ_GLU)
    u_ref[...] = glu[:, :D_CONV] * jax.nn.sigmoid(glu[:, D_CONV:])
    z_ref[...] = _project(h, w_ref, PROJ_Z)
    xbc_ref[...] = _project(h, w_ref, PROJ_XBC)
    dt_ref[...] = jax.nn.softplus(_project(h, w_ref, PROJ_DT) + dtb_ref[...])
    gate_ref[...] = jax.nn.sigmoid(_project(h, w_ref, PROJ_GATE))


def _proj_call(x, ng, w, dtb, *, tm):
    t = x.shape[0]
    row = lambda w_: pl.BlockSpec((tm, w_), lambda i: (i, 0))
    widths = (D_CONV, D_INNER, D_XBC, LANES, 2 * D_MODEL)
    return pl.pallas_call(
        _proj_kernel,
        grid=(t // tm,),
        in_specs=[row(D_MODEL)] + [_whole()] * 3,
        out_specs=[row(w_) for w_ in widths],
        out_shape=[jax.ShapeDtypeStruct((t, w_), F32) for w_ in widths],
        compiler_params=pltpu.CompilerParams(
            dimension_semantics=("arbitrary",), vmem_limit_bytes=VMEM_LIMIT),
        name="in_proj",
    )(x, ng, w, dtb)


def _store_time_major(dst_ref, t0, src, n, nb):
    for j in range(nb):
        dst_ref[pl.ds(t0 * nb + j, n, stride=nb), :] = src[:, j * LANES:(j + 1) * LANES]


def _load_time_major(dst_ref, src_ref, t0, n, nb):
    for j in range(nb):
        dst_ref[:, j * LANES:(j + 1) * LANES] = src_ref[pl.ds(t0 * nb + j, n, stride=nb), :]


def _conv_time_major(ext_ref, w_ref, b_ref, out_ref, *, n, nb, taps, lag0, steps, act=None,
                     start=0):
    steps = min(steps, n)
    rep = lambda tile: jnp.concatenate([tile] * steps, axis=0)
    for t0 in range(start, start + n, steps):
        acc = rep(b_ref[...])
        for k in range(taps):
            lo = (t0 + lag0 + k) * nb
            acc = acc + rep(w_ref[k * nb:(k + 1) * nb, :]) * ext_ref[lo:lo + steps * nb, :]
        out_ref[t0 * nb:(t0 + steps) * nb, :] = acc if act is None else act(acc)


def _silu(x):
    return x * jax.nn.sigmoid(x)


def _zero_after(x):
    bits = pltpu.bitcast(x[0:SUBLANES, 0:LANES], jnp.int32)
    return lax.shift_right_logical(lax.shift_right_logical(bits, 16), 16).astype(F32)


def _conv_a_kernel(u_ref, st_ref, w_ref, b_ref, lng_ref, lnb_ref, wpw_ref,
                   y_ref, nst_ref, ext_ref, out_ref, v_ref, *, tm):
    l = pl.program_id(1)
    nb = D_CONV // LANES

    @pl.when(l == 0)
    def _():
        _store_time_major(ext_ref, 0, st_ref, CONV_A_PAD, nb)

    _store_time_major(ext_ref, CONV_A_PAD, u_ref, tm, nb)
    _conv_time_major(ext_ref, w_ref, b_ref, out_ref, n=tm, nb=nb, taps=CONV_A_WIDTH,
                     lag0=CONV_A_PAD - (CONV_A_WIDTH - 1), steps=CONV_A_STEPS)
    _load_time_major(nst_ref, ext_ref, tm, CONV_A_PAD, nb)
    ext_ref[0:CONV_A_PAD * nb, :] = ext_ref[tm * nb:(tm + CONV_A_PAD) * nb, :]
    _load_time_major(v_ref, out_ref, 0, tm, nb)

    v = v_ref[...]
    mu = jnp.mean(v, axis=-1, keepdims=True)
    vc = v - mu
    var = jnp.mean(vc * vc, axis=-1, keepdims=True)
    vn = vc * lax.rsqrt(var + LN_EPS) * lng_ref[...] + lnb_ref[...]
    y_ref[...] = _dot((vn * jax.nn.sigmoid(vn)).astype(BF16), wpw_ref[...])


def _conv_a_call(u, st, w, b, lng, lnb, wpw, *, tm):
    bsz, seq, _ = u.shape
    tile = lambda w_: pl.BlockSpec((None, tm, w_), lambda i, j: (i, j, 0))
    state = pl.BlockSpec((None, CONV_A_PAD, D_CONV), lambda i, j: (i, 0, 0))
    return pl.pallas_call(
        functools.partial(_conv_a_kernel, tm=tm),
        grid=(bsz, seq // tm),
        in_specs=[tile(D_CONV), state] + [_whole()] * 5,
        out_specs=[tile(D_MODEL), state],
        out_shape=[jax.ShapeDtypeStruct((bsz, seq, D_MODEL), F32),
                   jax.ShapeDtypeStruct((bsz, CONV_A_PAD, D_CONV), F32)],
        scratch_shapes=[pltpu.VMEM(((tm + CONV_A_PAD) * (D_CONV // LANES), LANES), F32),
                        pltpu.VMEM((tm * (D_CONV // LANES), LANES), F32),
                        pltpu.VMEM((tm, D_CONV), F32)],
        compiler_params=pltpu.CompilerParams(
            dimension_semantics=("arbitrary", "arbitrary"), vmem_limit_bytes=VMEM_LIMIT),
        name="conv_a",
    )(u, st, w, b, lng, lnb, wpw)


def _layer_norm_swish(v, g, b):
    mu = jnp.mean(v, axis=-1, keepdims=True)
    vc = v - mu
    var = jnp.mean(vc * vc, axis=-1, keepdims=True)
    return _silu(vc * lax.rsqrt(var + LN_EPS) * g + b)


def _proj_conv_kernel(x_ref, st_ref, ng_ref, w_ref, dtb_ref, cw_ref, cb_ref, lng_ref, lnb_ref,
                      wpw_ref,
                      ya_ref, nst_ref, z_ref, xbc_ref, dt_ref, gate_ref,
                      ext_ref, out_ref, v_ref, h_ref, *, tm):
    l = pl.program_id(1)
    nb = D_CONV // LANES

    @pl.when(l == 0)
    def _():
        _store_time_major(ext_ref, 0, st_ref, CONV_A_PAD, nb)

    h = _rms_norm(x_ref[...], ng_ref[...]).astype(BF16)
    glu = _project(h, w_ref, PROJ_GLU)
    _store_time_major(ext_ref, CONV_A_PAD, glu[:, :D_CONV] * jax.nn.sigmoid(glu[:, D_CONV:]),
                      tm, nb)
    dt_ref[...] = jax.nn.softplus(_project(h, w_ref, PROJ_DT) + dtb_ref[...])
    h_ref[...] = h
    width = PROJ_PANEL
    panels = ([(z_ref, PROJ_Z[0] + c, c, None) for c in range(0, D_INNER, width)]
              + [(xbc_ref, PROJ_XBC[0] + c, c, None) for c in range(0, D_XBC, width)]
              + [(gate_ref, PROJ_GATE[0] + c, c, jax.nn.sigmoid)
                 for c in range(0, 2 * D_MODEL, width)])
    steps = CONV_A_STEPS
    n_blocks = tm // steps
    done = 0
    for i in range(n_blocks):
        want = (len(panels) * (i + 1)) // n_blocks
        tail = jnp.zeros((SUBLANES, LANES), F32)
        for dst, wc, oc, act in panels[done:want]:
            r = _dot(h_ref[...], w_ref[:, wc:wc + width])
            dst[:, oc:oc + width] = r if act is None else act(r)
            tail = tail + _zero_after(r)
        done = want
        tail = jnp.concatenate([tail] * (steps * nb // SUBLANES), axis=0)
        _conv_time_major(ext_ref, cw_ref, cb_ref, out_ref, start=i * steps, n=steps, nb=nb,
                         taps=CONV_A_WIDTH, lag0=CONV_A_PAD - (CONV_A_WIDTH - 1), steps=steps,
                         act=lambda a: a + tail)
    _load_time_major(nst_ref, ext_ref, tm, CONV_A_PAD, nb)
    ext_ref[0:CONV_A_PAD * nb, :] = ext_ref[tm * nb:(tm + CONV_A_PAD) * nb, :]
    _load_time_major(v_ref, out_ref, 0, tm, nb)
    ya_ref[...] = _dot(_layer_norm_swish(v_ref[...], lng_ref[...], lnb_ref[...]).astype(BF16),
                       wpw_ref[...])


def _proj_conv_call(x, st, proj_w, conv_w, *, tm):
    bsz, seq, _ = x.shape
    nb = D_CONV // LANES
    tile = lambda w_: pl.BlockSpec((None, tm, w_), lambda i, j: (i, j, 0))
    state = pl.BlockSpec((None, CONV_A_PAD, D_CONV), lambda i, j: (i, 0, 0))
    widths = (D_MODEL, D_INNER, D_XBC, LANES, 2 * D_MODEL)
    outs = [jax.ShapeDtypeStruct((bsz, seq, w), F32) for w in widths]
    outs.insert(1, jax.ShapeDtypeStruct((bsz, CONV_A_PAD, D_CONV), F32))
    out_specs = [tile(w) for w in widths]
    out_specs.insert(1, state)
    return pl.pallas_call(
        functools.partial(_proj_conv_kernel, tm=tm),
        grid=(bsz, seq // tm),
        in_specs=[tile(D_MODEL), state] + [_whole()] * (len(proj_w) + len(conv_w)),
        out_specs=out_specs,
        out_shape=outs,
        scratch_shapes=[pltpu.VMEM(((tm + CONV_A_PAD) * nb, LANES), F32),
                        pltpu.VMEM((tm * nb, LANES), F32),
                        pltpu.VMEM((tm, D_CONV), F32),
                        pltpu.VMEM((tm, D_MODEL), BF16)],
        compiler_params=pltpu.CompilerParams(
            dimension_semantics=("arbitrary", "arbitrary"), vmem_limit_bytes=VMEM_LIMIT),
        name="proj_conv",
    )(x, st, *proj_w, *conv_w)


def _cumsum_rows(a):
    q = a.shape[0]
    row = lax.broadcasted_iota(jnp.int32, a.shape, 0)
    s = 1
    while s < q:
        a = a + jnp.where(row >= s, pltpu.roll(a, s, axis=0), 0.0)
        s *= 2
    return a


def _expand_heads(v, e_ref):
    lane = lax.broadcasted_iota(jnp.int32, v.shape, 1)
    hi = v.astype(BF16)
    rest = v - hi.astype(F32)
    mid = rest.astype(BF16)
    lo = (rest - mid.astype(F32)).astype(BF16)
    terms = jnp.where(lane < N_HEADS, hi, jnp.where(lane < 2 * N_HEADS, mid, lo))
    return _dot(terms, e_ref[...])


def _ssd_kernel(xbc_ref, z_ref, dt_ref, cst_ref, hst_ref, w_ref, b_ref, alog_ref, dskip_ref,
                ng_ref, wout_ref, expand_ref,
                y_ref, ncst_ref, nhst_ref,
                ext_ref, cout_ref, xc_ref, ht_ref, yg_ref, *, tm, rows_in, n_l):
    l = pl.program_id(1)
    q = SSD_CHUNK
    nb = D_XBC // LANES

    @pl.when(l == 0)
    def _():
        _store_time_major(ext_ref, 0, cst_ref, CONV_B_PAD, nb)
        for g in range(N_GROUPS):
            ht_ref[g] = hst_ref[g].T

    if rows_in < tm:
        ext_ref[(CONV_B_PAD + rows_in) * nb:(CONV_B_PAD + tm) * nb, :] = jnp.zeros(
            ((tm - rows_in) * nb, LANES), F32)
    _store_time_major(ext_ref, CONV_B_PAD, xbc_ref, rows_in, nb)
    _load_time_major(ncst_ref, ext_ref, rows_in, CONV_B_PAD, nb)

    a_row = -jnp.exp(alog_ref[...])
    head_ok = lax.broadcasted_iota(jnp.int32, (1, LANES), 1) < HEAD_COPIES * N_HEADS
    a_row = jnp.where(head_ok, a_row, 0.0)
    causal = (lax.broadcasted_iota(jnp.int32, (q, q), 0)
              >= lax.broadcasted_iota(jnp.int32, (q, q), 1))
    not_causal = jnp.where(causal, 0.0, NEG_BIG)
    lane_lo = lax.broadcasted_iota(jnp.int32, (q, LANES), 1) < HEAD_DIM
    zero_b = jnp.zeros((q, LANES), BF16)

    for c0 in range(0, tm, q):
        rows = min(q, rows_in - c0)
        _conv_time_major(ext_ref, w_ref, b_ref, cout_ref, start=c0, n=q, nb=nb, taps=CONV_B_WIDTH,
                         lag0=CONV_B_PAD - (CONV_B_WIDTH - 1), steps=8, act=_silu)
        _load_time_major(xc_ref, cout_ref, c0, q, nb)

        if rows == q:
            dt = dt_ref[c0:c0 + q, :]
        else:
            dt = jnp.concatenate([dt_ref[c0:c0 + rows, :], jnp.zeros((q - rows, LANES), F32)],
                                 axis=0)
        acum = _cumsum_rows(dt * a_row)
        a_last = acum[q - 1:q, :]
        e_acum = jnp.exp(acum)
        s_tail = dt * jnp.exp(a_last - acum)
        acum2 = acum * LOG2_E
        col2 = acum2
        row2_t = (acum2 - jnp.log2(dt)).T

        expanded = _expand_heads(jnp.concatenate([e_acum, s_tail], axis=0), expand_ref)
        e_acum_x = expanded[0:q]
        s_tail_x = expanded[q:2 * q]
        e_last_x = expanded[q - 1:q]

        xs = xc_ref[:, 0:D_INNER]
        w_all = (xs * s_tail_x).astype(BF16)
        xs_b = xs.astype(BF16)

        for g in range(N_GROUPS):
            b_g = xc_ref[:, D_INNER + g * D_STATE:D_INNER + (g + 1) * D_STATE]
            c_g = xc_ref[:, D_INNER + (N_GROUPS + g) * D_STATE:
                         D_INNER + (N_GROUPS + g + 1) * D_STATE]
            c_gb = c_g.astype(BF16)
            cb = lax.dot_general(c_gb, b_g.astype(BF16), (((1,), (1,)), ((), ())),
                                 preferred_element_type=F32)
            gs = slice(g * GROUP_WIDTH, (g + 1) * GROUP_WIDTH)
            ht = ht_ref[g]
            y_g = _dot(c_gb, ht.astype(BF16)) * e_acum_x[:, gs]
            for pr in range(HEADS_PER_GROUP // 2):
                ms = []
                for hh in range(2):
                    hd = g * HEADS_PER_GROUP + 2 * pr + hh
                    seg2 = col2[:, hd:hd + 1] - row2_t[hd:hd + 1, :] + not_causal
                    ms.append((cb * jnp.exp2(seg2)).astype(BF16))
                lhs = jnp.concatenate(ms, axis=1)
                cols = slice(g * GROUP_WIDTH + pr * LANES, g * GROUP_WIDTH + (pr + 1) * LANES)
                xp = xs_b[:, cols]
                rhs = jnp.concatenate([jnp.where(lane_lo, xp, zero_b),
                                       jnp.where(lane_lo, zero_b, xp)], axis=0)
                yg_ref[:, cols] = _dot(lhs, rhs) + y_g[:, pr * LANES:(pr + 1) * LANES]
            ht_ref[g] = e_last_x[:, gs] * ht + _dot(b_g.T.astype(BF16), w_all[:, gs])

        z = z_ref[c0:c0 + rows, :]
        gated = (yg_ref[0:rows, :] + dskip_ref[...] * xs[0:rows]) * _silu(z)
        parts = []
        for g in range(N_GROUPS):
            gg = gated[:, g * GROUP_WIDTH:(g + 1) * GROUP_WIDTH]
            ms = jnp.mean(gg * gg, axis=-1, keepdims=True)
            parts.append(gg * lax.rsqrt(ms + RMS_EPS))
        normed = jnp.concatenate(parts, axis=1) * ng_ref[...]
        y_ref[c0:c0 + rows, :] = _dot(normed.astype(BF16), wout_ref[...])

    ext_ref[0:CONV_B_PAD * nb, :] = ext_ref[rows_in * nb:(rows_in + CONV_B_PAD) * nb, :]

    @pl.when(l == n_l - 1)
    def _():
        for g in range(N_GROUPS):
            nhst_ref[g] = ht_ref[g].T


def _ssd_call(xbc, z, dt, cst, hst, w, b, alog, dskip, ng, wout, expand, *, tm):
    bsz, seq, _ = xbc.shape
    rows_in = min(tm, seq)
    tm_k = max(tm, SSD_CHUNK) if seq < SSD_CHUNK else tm
    n_l = seq // rows_in
    tile = lambda w_: pl.BlockSpec((None, rows_in, w_), lambda i, j: (i, j, 0))
    cstate = pl.BlockSpec((None, CONV_B_PAD, D_XBC), lambda i, j: (i, 0, 0))
    hstate = pl.BlockSpec((None, N_GROUPS, GROUP_WIDTH, D_STATE), lambda i, j: (i, 0, 0, 0))
    return pl.pallas_call(
        functools.partial(_ssd_kernel, tm=tm_k, rows_in=rows_in, n_l=n_l),
        grid=(bsz, n_l),
        in_specs=[tile(D_XBC), tile(D_INNER), tile(LANES), cstate, hstate] + [_whole()] * 7,
        out_specs=[tile(D_MODEL), cstate, hstate],
        out_shape=[jax.ShapeDtypeStruct((bsz, seq, D_MODEL), F32),
                   jax.ShapeDtypeStruct((bsz, CONV_B_PAD, D_XBC), F32),
                   jax.ShapeDtypeStruct((bsz, N_GROUPS, GROUP_WIDTH, D_STATE), F32)],
        scratch_shapes=[pltpu.VMEM(((tm_k + CONV_B_PAD) * (D_XBC // LANES), LANES), F32),
                        pltpu.VMEM((tm_k * (D_XBC // LANES), LANES), F32),
                        pltpu.VMEM((SSD_CHUNK, D_XBC), F32),
                        pltpu.VMEM((N_GROUPS, D_STATE, GROUP_WIDTH), F32),
                        pltpu.VMEM((SSD_CHUNK, D_INNER), F32)],
        compiler_params=pltpu.CompilerParams(
            dimension_semantics=("arbitrary", "arbitrary"), vmem_limit_bytes=VMEM_LIMIT),
        name="ssd",
    )(xbc, z, dt, cst, hst, w, b, alog, dskip, ng, wout, expand)


def _pad_rows_front(a, rows):
    pad = rows - a.shape[-2]
    return jnp.pad(a, [(0, 0)] * (a.ndim - 2) + [(pad, 0), (0, 0)])


def _prepare_params(p):
    w_in = p["w_in"][0]
    s2 = 2 * D_CONV + D_INNER + D_XBC
    s3 = s2 + N_HEADS
    row = lambda v: v.reshape(1, -1).astype(F32)
    lane_pad = lambda v: jnp.pad(jnp.tile(v, (1, HEAD_COPIES)),
                                 ((0, 0), (0, LANES - HEAD_COPIES * v.shape[1])))
    lane_group = np.arange(LANES)[:, None]
    expand = ((lane_group % N_HEADS == np.arange(D_INNER)[None, :] // HEAD_DIM)
              & (lane_group < HEAD_COPIES * N_HEADS))
    q = {}
    for name in ("ffn1", "ffn2"):
        q[name] = (row(p["norm_" + name][0]), p[name + "_w_gate"][0].astype(BF16),
                   p[name + "_w_up"][0].astype(BF16), p[name + "_w_down"][0].astype(BF16))
    w_proj = jnp.concatenate([w_in[:, :s2], w_in[:, s3:], lane_pad(w_in[:, s2:s3])], axis=1)
    q["proj"] = (row(p["norm_mix"][0]), w_proj.astype(BF16), lane_pad(row(p["dt_bias"][0])))
    q["conv_a"] = (p["conv_a_w"][0].reshape(-1, LANES), p["conv_a_b"][0].reshape(-1, LANES),
                   row(p["ln_conv_g"][0]), row(p["ln_conv_b"][0]),
                   p["w_conv_a_out"][0].astype(BF16))
    q["ssd"] = (p["conv_b_w"][0].reshape(-1, LANES), p["conv_b_b"][0].reshape(-1, LANES),
                lane_pad(row(p["a_log"][0])),
                row(jnp.repeat(p["d_skip"][0], HEAD_DIM)), row(p["ssm_norm_g"][0]),
                p["w_ssm_out"][0].astype(BF16), jnp.asarray(expand, BF16))
    q["w_out"] = p["w_out"][0].astype(BF16)
    q["norm_final"] = row(p["norm_final"])
    return q


def _encoder(x, st_a, st_b, st_h, q, *, tm_ffn, tm_mix):
    bsz, seq, _ = x.shape
    t = bsz * seq
    x0 = x.reshape(t, D_MODEL)
    x1 = _ffn_call(x0, *q["ffn1"], tm=min(tm_ffn, t))
    tm_seq = min(tm_mix, seq)
    shp = lambda a: a.reshape(bsz, seq, a.shape[-1])
    st_a = _pad_rows_front(st_a, CONV_A_PAD)
    if seq >= tm_mix:
        y_a, new_a, z, xbc, dt, gates = _proj_conv_call(shp(x1), st_a, q["proj"], q["conv_a"],
                                                        tm=tm_seq)
        gates = gates.reshape(t, 2 * D_MODEL)
    else:
        u, z, xbc, dt, gates = _proj_call(x1, *q["proj"], tm=min(tm_mix, t))
        y_a, new_a = _conv_a_call(shp(u), st_a, *q["conv_a"], tm=tm_seq)
    y_b, new_b, new_h = _ssd_call(
        shp(xbc), shp(z), shp(dt), _pad_rows_front(st_b, CONV_B_PAD),
        st_h.reshape(bsz, N_GROUPS, GROUP_WIDTH, D_STATE), *q["ssd"], tm=tm_seq)
    y = _ffn_call(x1, *q["ffn2"], tm=min(tm_mix, t),
                  merge=(y_a.reshape(t, D_MODEL), y_b.reshape(t, D_MODEL), gates, q["w_out"]),
                  final_g=q["norm_final"])
    return (y.reshape(bsz, seq, D_MODEL),
            new_a[None, :, CONV_A_PAD - (CONV_A_WIDTH - 1):],
            new_b[None, :, CONV_B_PAD - (CONV_B_WIDTH - 1):],
            new_h.reshape(1, bsz, N_HEADS, HEAD_DIM, D_STATE))


def kernel(x_prompt, x_sample, state_conv_a, state_conv_b, state_ssm, norm_ffn1, ffn1_w_gate, ffn1_w_up, ffn1_w_down, norm_mix, w_in, conv_a_w, conv_a_b, ln_conv_g, ln_conv_b, w_conv_a_out, conv_b_w, conv_b_b, dt_bias, a_log, d_skip, ssm_norm_g, w_ssm_out, w_out, norm_ffn2, ffn2_w_gate, ffn2_w_up, ffn2_w_down, norm_final):
    p = dict(norm_ffn1=norm_ffn1, ffn1_w_gate=ffn1_w_gate, ffn1_w_up=ffn1_w_up,
             ffn1_w_down=ffn1_w_down, norm_mix=norm_mix, w_in=w_in, conv_a_w=conv_a_w,
             conv_a_b=conv_a_b, ln_conv_g=ln_conv_g, ln_conv_b=ln_conv_b,
             w_conv_a_out=w_conv_a_out, conv_b_w=conv_b_w, conv_b_b=conv_b_b, dt_bias=dt_bias,
             a_log=a_log, d_skip=d_skip, ssm_norm_g=ssm_norm_g, w_ssm_out=w_ssm_out, w_out=w_out,
             norm_ffn2=norm_ffn2, ffn2_w_gate=ffn2_w_gate, ffn2_w_up=ffn2_w_up,
             ffn2_w_down=ffn2_w_down, norm_final=norm_final)
    q = _prepare_params(p)
    bp = x_prompt.shape[0]
    zero_a = jnp.zeros((bp, CONV_A_WIDTH - 1, D_CONV), F32)
    zero_b = jnp.zeros((bp, CONV_B_WIDTH - 1, D_XBC), F32)
    zero_h = jnp.zeros((bp, N_HEADS, HEAD_DIM, D_STATE), F32)
    y_p, a_p, b_p, h_p = _encoder(x_prompt, zero_a, zero_b, zero_h, q, tm_ffn=512, tm_mix=256)
    y_s, a_s, b_s, h_s = _encoder(x_sample, state_conv_a[0], state_conv_b[0], state_ssm[0], q,
                                  tm_ffn=256, tm_mix=256)
    return (y_p, y_s, a_p, b_p, h_p, a_s, b_s, h_s)
```

```python
import functools

import jax
import jax.numpy as jnp
import numpy as np
from jax import lax
from jax.experimental import pallas as pl
from jax.experimental.pallas import tpu as pltpu

D_MODEL = 1024
D_FF = 2816
D_CONV = D_MODEL
CONV_A_WIDTH = 31
D_INNER = 2048
HEAD_DIM = 64
N_HEADS = D_INNER // HEAD_DIM
N_GROUPS = 4
HEADS_PER_GROUP = N_HEADS // N_GROUPS
GROUP_WIDTH = D_INNER // N_GROUPS
D_STATE = 128
CONV_B_WIDTH = 4
D_XBC = D_INNER + 2 * N_GROUPS * D_STATE
RMS_EPS = 1e-6
LN_EPS = 1e-5
FFN_RES_WEIGHT = 0.5

LANES = 128
SUBLANES = 8
CONV_A_PAD = 32
CONV_B_PAD = 8
SSD_CHUNK = 128
HEAD_COPIES = 3
FF_CHUNK = 256
PROJ_PANEL = 256
CONV_A_STEPS = 16
VMEM_LIMIT = 56 * 1024 * 1024
NEG_BIG = -1e30
LOG2_E = 1.4426950408889634

F32 = jnp.float32
BF16 = jnp.bfloat16


def _dot(a, b):
    return jnp.dot(a, b, preferred_element_type=F32)


def _rms_norm(x, g):
    ms = jnp.mean(x * x, axis=-1, keepdims=True)
    return x * lax.rsqrt(ms + RMS_EPS) * g


def _whole():
    return pl.BlockSpec(memory_space=pltpu.VMEM)


def _ffn_kernel(*refs, merge, final):
    it = iter(refs)
    x_ref = next(it)
    if merge:
        ma_ref, mb_ref, wo_ref = next(it), next(it), next(it)
    ng_ref, wg_ref, wu_ref, wd_ref = next(it), next(it), next(it), next(it)
    if final:
        fg_ref = next(it)
    o_ref = next(it)
    a_scr = next(it)

    x = x_ref[...]
    if merge:
        x = x + _dot((ma_ref[...] + mb_ref[...]).astype(BF16), wo_ref[...])
    h = _rms_norm(x, ng_ref[...]).astype(BF16)
    for c in range(0, D_FF, FF_CHUNK):
        g = _dot(h, wg_ref[:, c:c + FF_CHUNK])
        u = _dot(h, wu_ref[:, c:c + FF_CHUNK])
        a_scr[:, c:c + FF_CHUNK] = (g * jax.nn.sigmoid(g) * u).astype(BF16)
    y = x + FFN_RES_WEIGHT * _dot(a_scr[...], wd_ref[...])
    if final:
        y = _rms_norm(y, fg_ref[...])
    o_ref[...] = y


def _ffn_call(x, ng, wg, wu, wd, *, tm, merge=None, final_g=None):
    t = x.shape[0]
    row = lambda w: pl.BlockSpec((tm, w), lambda i: (i, 0))
    args, specs = [x], [row(D_MODEL)]
    if merge is not None:
        ma, mb, wo = merge
        args += [ma, mb, wo]
        specs += [row(D_MODEL), row(D_MODEL), _whole()]
    args += [ng, wg, wu, wd]
    specs += [_whole()] * 4
    if final_g is not None:
        args.append(final_g)
        specs.append(_whole())
    return pl.pallas_call(
        functools.partial(_ffn_kernel, merge=merge is not None, final=final_g is not None),
        grid=(t // tm,),
        in_specs=specs,
        out_specs=row(D_MODEL),
        out_shape=jax.ShapeDtypeStruct((t, D_MODEL), F32),
        scratch_shapes=[pltpu.VMEM((tm, D_FF), BF16)],
        compiler_params=pltpu.CompilerParams(
            dimension_semantics=("arbitrary",), vmem_limit_bytes=VMEM_LIMIT),
        name="ffn_merge" if merge is not None else "ffn",
    )(*args)


def _gated_matmul_kernel(a_ref, w_ref, g_ref, o_ref):
    o_ref[...] = g_ref[...] * _dot(a_ref[...], w_ref[...])


def _gated_matmul_call(a, w, g):
    return pl.pallas_call(
        _gated_matmul_kernel,
        in_specs=[_whole()] * 3,
        out_specs=_whole(),
        out_shape=jax.ShapeDtypeStruct(g.shape, F32),
        compiler_params=pltpu.CompilerParams(vmem_limit_bytes=VMEM_LIMIT),
        name="gated_matmul",
    )(a, w, g)


PROJ_GLU = (0, 2 * D_CONV)
PROJ_Z = (PROJ_GLU[1], PROJ_GLU[1] + D_INNER)
PROJ_XBC = (PROJ_Z[1], PROJ_Z[1] + D_XBC)
PROJ_GATE = (PROJ_XBC[1], PROJ_XBC[1] + 2 * D_MODEL)
PROJ_DT = (PROJ_GATE[1], PROJ_GATE[1] + LANES)


def _project(h, wt_ref, seg):
    return lax.dot_general(h, wt_ref[seg[0]:seg[1], :], (((1,), (1,)), ((), ())),
                           preferred_element_type=F32)


def _proj_kernel(x_ref, ng_ref, w_ref, dtb_ref, u_ref, z_ref, xbc_ref, dt_ref, ga_ref, gb_ref):
    h = _rms_norm(x_ref[...], ng_ref[...]).astype(BF16)
    glu = _project(h, w_ref, PROJ_GLU)
    u_ref[...] = glu[:, :D_CONV] * jax.nn.sigmoid(glu[:, D_CONV:])
    z_ref[...] = _project(h, w_ref, PROJ_Z)
    xbc_ref[...] = _project(h, w_ref, PROJ_XBC)
    dt_ref[...] = jax.nn.softplus(_project(h, w_ref, PROJ_DT) + dtb_ref[...])
    gates = jax.nn.sigmoid(_project(h, w_ref, PROJ_GATE))
    ga_ref[...] = gates[:, :D_MODEL]
    gb_ref[...] = gates[:, D_MODEL:]


def _proj_call(x, ng, w, dtb, *, tm):
    t = x.shape[0]
    row = lambda w_: pl.BlockSpec((tm, w_), lambda i: (i, 0))
    widths = (D_CONV, D_INNER, D_XBC, LANES, D_MODEL, D_MODEL)
    return pl.pallas_call(
        _proj_kernel,
        grid=(t // tm,),
        in_specs=[row(D_MODEL)] + [_whole()] * 3,
        out_specs=[row(w_) for w_ in widths],
        out_shape=[jax.ShapeDtypeStruct((t, w_), F32) for w_ in widths],
        compiler_params=pltpu.CompilerParams(
            dimension_semantics=("arbitrary",), vmem_limit_bytes=VMEM_LIMIT),
        name="in_proj",
    )(x, ng, w, dtb)


def _store_time_major(dst_ref, t0, src, n, nb):
    for j in range(nb):
        dst_ref[pl.ds(t0 * nb + j, n, stride=nb), :] = src[:, j * LANES:(j + 1) * LANES]


def _load_time_major(dst_ref, src_ref, t0, n, nb):
    for j in range(nb):
        dst_ref[:, j * LANES:(j + 1) * LANES] = src_ref[pl.ds(t0 * nb + j, n, stride=nb), :]


def _conv_time_major(ext_ref, w_ref, b_ref, out_ref, *, n, nb, taps, lag0, steps, act=None,
                     start=0):
    steps = min(steps, n)
    rep = lambda tile: jnp.concatenate([tile] * steps, axis=0)
    for t0 in range(start, start + n, steps):
        acc = rep(b_ref[...])
        for k in range(taps):
            lo = (t0 + lag0 + k) * nb
            acc = acc + rep(w_ref[k * nb:(k + 1) * nb, :]) * ext_ref[lo:lo + steps * nb, :]
        out_ref[t0 * nb:(t0 + steps) * nb, :] = acc if act is None else act(acc)


def _silu(x):
    return x * jax.nn.sigmoid(x)


def _zero_after(x):
    bits = pltpu.bitcast(x[0:SUBLANES, 0:LANES], jnp.int32)
    return lax.shift_right_logical(lax.shift_right_logical(bits, 16), 16).astype(F32)


def _layer_norm_swish(v, g, b):
    mu = jnp.mean(v, axis=-1, keepdims=True)
    vc = v - mu
    var = jnp.mean(vc * vc, axis=-1, keepdims=True)
    return _silu(vc * lax.rsqrt(var + LN_EPS) * g + b)


def _conv_a_kernel(u_ref, st_ref, w_ref, b_ref, lng_ref, lnb_ref,
                   act_ref, nst_ref, ext_ref, out_ref, v_ref, *, tm):
    l = pl.program_id(1)
    nb = D_CONV // LANES

    @pl.when(l == 0)
    def _():
        _store_time_major(ext_ref, 0, st_ref, CONV_A_PAD, nb)

    _store_time_major(ext_ref, CONV_A_PAD, u_ref, tm, nb)
    _conv_time_major(ext_ref, w_ref, b_ref, out_ref, n=tm, nb=nb, taps=CONV_A_WIDTH,
                     lag0=CONV_A_PAD - (CONV_A_WIDTH - 1), steps=CONV_A_STEPS)
    _load_time_major(nst_ref, ext_ref, tm, CONV_A_PAD, nb)
    ext_ref[0:CONV_A_PAD * nb, :] = ext_ref[tm * nb:(tm + CONV_A_PAD) * nb, :]
    _load_time_major(v_ref, out_ref, 0, tm, nb)

    act_ref[...] = _layer_norm_swish(v_ref[...], lng_ref[...], lnb_ref[...]).astype(BF16)


def _conv_a_call(u, st, w, b, lng, lnb, *, tm):
    bsz, seq, _ = u.shape
    tile = lambda w_: pl.BlockSpec((None, tm, w_), lambda i, j: (i, j, 0))
    state = pl.BlockSpec((None, CONV_A_PAD, D_CONV), lambda i, j: (i, 0, 0))
    return pl.pallas_call(
        functools.partial(_conv_a_kernel, tm=tm),
        grid=(bsz, seq // tm),
        in_specs=[tile(D_CONV), state] + [_whole()] * 4,
        out_specs=[tile(D_CONV), state],
        out_shape=[jax.ShapeDtypeStruct((bsz, seq, D_CONV), BF16),
                   jax.ShapeDtypeStruct((bsz, CONV_A_PAD, D_CONV), F32)],
        scratch_shapes=[pltpu.VMEM(((tm + CONV_A_PAD) * (D_CONV // LANES), LANES), F32),
                        pltpu.VMEM((tm * (D_CONV // LANES), LANES), F32),
                        pltpu.VMEM((tm, D_CONV), F32)],
        compiler_params=pltpu.CompilerParams(
            dimension_semantics=("arbitrary", "arbitrary"), vmem_limit_bytes=VMEM_LIMIT),
        name="conv_a",
    )(u, st, w, b, lng, lnb)


def _proj_conv_kernel(x_ref, st_ref, ng_ref, w_ref, dtb_ref, cw_ref, cb_ref, lng_ref, lnb_ref,
                      wpw_ref,
                      ma_ref, nst_ref, z_ref, xbc_ref, dt_ref, gb_ref,
                      ext_ref, out_ref, v_ref, h_ref, ga_ref, *, tm):
    l = pl.program_id(1)
    nb = D_CONV // LANES

    @pl.when(l == 0)
    def _():
        _store_time_major(ext_ref, 0, st_ref, CONV_A_PAD, nb)

    h = _rms_norm(x_ref[...], ng_ref[...]).astype(BF16)
    glu = _project(h, w_ref, PROJ_GLU)
    _store_time_major(ext_ref, CONV_A_PAD, glu[:, :D_CONV] * jax.nn.sigmoid(glu[:, D_CONV:]),
                      tm, nb)
    dt_ref[...] = jax.nn.softplus(_project(h, w_ref, PROJ_DT) + dtb_ref[...])
    h_ref[...] = h
    width = PROJ_PANEL
    panels = ([(z_ref, PROJ_Z[0] + c, c, None) for c in range(0, D_INNER, width)]
              + [(xbc_ref, PROJ_XBC[0] + c, c, None) for c in range(0, D_XBC, width)]
              + [(ga_ref, PROJ_GATE[0] + c, c, jax.nn.sigmoid)
                 for c in range(0, D_MODEL, width)]
              + [(gb_ref, PROJ_GATE[0] + D_MODEL + c, c, jax.nn.sigmoid)
                 for c in range(0, D_MODEL, width)])
    steps = CONV_A_STEPS
    n_blocks = tm // steps
    done = 0
    for i in range(n_blocks):
        want = (len(panels) * (i + 1)) // n_blocks
        tail = jnp.zeros((SUBLANES, LANES), F32)
        for dst, wc, oc, act in panels[done:want]:
            r = _project(h_ref[...], w_ref, (wc, wc + width))
            dst[:, oc:oc + width] = r if act is None else act(r)
            tail = tail + _zero_after(r)
        done = want
        tail = jnp.concatenate([tail] * (steps * nb // SUBLANES), axis=0)
        _conv_time_major(ext_ref, cw_ref, cb_ref, out_ref, start=i * steps, n=steps, nb=nb,
                         taps=CONV_A_WIDTH, lag0=CONV_A_PAD - (CONV_A_WIDTH - 1), steps=steps,
                         act=lambda a: a + tail)
    _load_time_major(nst_ref, ext_ref, tm, CONV_A_PAD, nb)
    ext_ref[0:CONV_A_PAD * nb, :] = ext_ref[tm * nb:(tm + CONV_A_PAD) * nb, :]
    _load_time_major(v_ref, out_ref, 0, tm, nb)
    swished = _layer_norm_swish(v_ref[...], lng_ref[...], lnb_ref[...]).astype(BF16)
    ma_ref[...] = ga_ref[...] * _dot(swished, wpw_ref[...])


def _proj_conv_call(x, st, proj_w, conv_w, *, tm):
    bsz, seq, _ = x.shape
    nb = D_CONV // LANES
    tile = lambda w_: pl.BlockSpec((None, tm, w_), lambda i, j: (i, j, 0))
    state = pl.BlockSpec((None, CONV_A_PAD, D_CONV), lambda i, j: (i, 0, 0))
    widths = (D_MODEL, D_INNER, D_XBC, LANES, D_MODEL)
    outs = [jax.ShapeDtypeStruct((bsz, seq, w), F32) for w in widths]
    outs.insert(1, jax.ShapeDtypeStruct((bsz, CONV_A_PAD, D_CONV), F32))
    out_specs = [tile(w) for w in widths]
    out_specs.insert(1, state)
    return pl.pallas_call(
        functools.partial(_proj_conv_kernel, tm=tm),
        grid=(bsz, seq // tm),
        in_specs=[tile(D_MODEL), state] + [_whole()] * (len(proj_w) + len(conv_w)),
        out_specs=out_specs,
        out_shape=outs,
        scratch_shapes=[pltpu.VMEM(((tm + CONV_A_PAD) * nb, LANES), F32),
                        pltpu.VMEM((tm * nb, LANES), F32),
                        pltpu.VMEM((tm, D_CONV), F32),
                        pltpu.VMEM((tm, D_MODEL), BF16),
                        pltpu.VMEM((tm, D_MODEL), F32)],
        compiler_params=pltpu.CompilerParams(
            dimension_semantics=("arbitrary", "arbitrary"), vmem_limit_bytes=VMEM_LIMIT),
        name="proj_conv",
    )(x, st, *proj_w, *conv_w)


def _cumsum_rows(a):
    q = a.shape[0]
    row = lax.broadcasted_iota(jnp.int32, a.shape, 0)
    s = 1
    while s < q:
        a = a + jnp.where(row >= s, pltpu.roll(a, s, axis=0), 0.0)
        s *= 2
    return a


def _split_terms(v):
    lane = lax.broadcasted_iota(jnp.int32, v.shape, 1)
    hi = v.astype(BF16)
    rest = v - hi.astype(F32)
    mid = rest.astype(BF16)
    lo = (rest - mid.astype(F32)).astype(BF16)
    return jnp.where(lane < N_HEADS, hi, jnp.where(lane < 2 * N_HEADS, mid, lo))


def _ssd_kernel(*refs, tm, rows_in, n_l, project):
    it = iter(refs)
    xbc_ref, z_ref, dt_ref, cst_ref, hst_ref = (next(it) for _ in range(5))
    if project:
        gb_ref = next(it)
    w_ref, b_ref, alog_ref, dskip_ref, ng_ref, expand_ref = (next(it) for _ in range(6))
    if project:
        wout_ref = next(it)
    y_ref, ncst_ref, nhst_ref, ext_ref, cout_ref, xc_ref, ht_ref, gn_ref = (
        next(it) for _ in range(8))
    l = pl.program_id(1)
    q = SSD_CHUNK
    nb = D_XBC // LANES

    @pl.when(l == 0)
    def _():
        _store_time_major(ext_ref, 0, cst_ref, CONV_B_PAD, nb)
        for g in range(N_GROUPS):
            ht_ref[g] = hst_ref[g].T

    if rows_in < tm:
        ext_ref[(CONV_B_PAD + rows_in) * nb:(CONV_B_PAD + tm) * nb, :] = jnp.zeros(
            ((tm - rows_in) * nb, LANES), F32)
    _store_time_major(ext_ref, CONV_B_PAD, xbc_ref, rows_in, nb)
    _load_time_major(ncst_ref, ext_ref, rows_in, CONV_B_PAD, nb)

    a_row = -jnp.exp(alog_ref[...])
    head_ok = lax.broadcasted_iota(jnp.int32, (1, LANES), 1) < HEAD_COPIES * N_HEADS
    a_row = jnp.where(head_ok, a_row, 0.0)
    causal = (lax.broadcasted_iota(jnp.int32, (q, q), 0)
              >= lax.broadcasted_iota(jnp.int32, (q, q), 1))
    not_causal = jnp.where(causal, 0.0, NEG_BIG)
    lane_lo = lax.broadcasted_iota(jnp.int32, (q, LANES), 1) < HEAD_DIM
    zero_b = jnp.zeros((q, LANES), BF16)

    for c0 in range(0, tm, q):
        rows = min(q, rows_in - c0)
        _conv_time_major(ext_ref, w_ref, b_ref, cout_ref, start=c0, n=q, nb=nb, taps=CONV_B_WIDTH,
                         lag0=CONV_B_PAD - (CONV_B_WIDTH - 1), steps=8, act=_silu)
        _load_time_major(xc_ref, cout_ref, c0, q, nb)

        if rows == q:
            dt = dt_ref[c0:c0 + q, :]
        else:
            dt = jnp.concatenate([dt_ref[c0:c0 + rows, :], jnp.zeros((q - rows, LANES), F32)],
                                 axis=0)
        acum = _cumsum_rows(dt * a_row)
        a_last = acum[q - 1:q, :]
        e_acum = jnp.exp(acum)
        s_tail = dt * jnp.exp(a_last - acum)
        acum2 = acum * LOG2_E
        col2 = acum2
        row2_t = (acum2 - jnp.log2(dt)).T

        terms = _split_terms(jnp.concatenate([e_acum, s_tail], axis=0))

        for g in range(N_GROUPS):
            gs = slice(g * GROUP_WIDTH, (g + 1) * GROUP_WIDTH)
            expanded = _dot(terms, expand_ref[:, gs])
            e_acum_x = expanded[0:q]
            s_tail_x = expanded[q:2 * q]
            e_last_x = expanded[q - 1:q]
            xs = xc_ref[:, gs]
            xs_b = xs.astype(BF16)
            b_g = xc_ref[:, D_INNER + g * D_STATE:D_INNER + (g + 1) * D_STATE]
            c_g = xc_ref[:, D_INNER + (N_GROUPS + g) * D_STATE:
                         D_INNER + (N_GROUPS + g + 1) * D_STATE]
            c_gb = c_g.astype(BF16)
            cb = lax.dot_general(c_gb, b_g.astype(BF16), (((1,), (1,)), ((), ())),
                                 preferred_element_type=F32)
            ht = ht_ref[g]
            y_st = _dot(c_gb, ht.astype(BF16)) * e_acum_x
            y_in = []
            for pr in range(HEADS_PER_GROUP // 2):
                ms = []
                for hh in range(2):
                    hd = g * HEADS_PER_GROUP + 2 * pr + hh
                    seg2 = col2[:, hd:hd + 1] - row2_t[hd:hd + 1, :] + not_causal
                    ms.append((cb * jnp.exp2(seg2)).astype(BF16))
                lhs = jnp.concatenate(ms, axis=1)
                xp = xs_b[:, pr * LANES:(pr + 1) * LANES]
                rhs = jnp.concatenate([jnp.where(lane_lo, xp, zero_b),
                                       jnp.where(lane_lo, zero_b, xp)], axis=0)
                y_in.append(_dot(lhs, rhs))
            ht_ref[g] = e_last_x * ht + _dot(b_g.T.astype(BF16), (xs * s_tail_x).astype(BF16))

            y = jnp.concatenate(y_in, axis=1) + y_st
            gated = ((y + dskip_ref[:, gs] * xs)[0:rows] * _silu(z_ref[c0:c0 + rows, gs]))
            ms = jnp.mean(gated * gated, axis=-1, keepdims=True)
            gn_ref[0:rows, gs] = (gated * lax.rsqrt(ms + RMS_EPS) * ng_ref[:, gs]).astype(BF16)

        if project:
            y_ref[c0:c0 + rows, :] = gb_ref[c0:c0 + rows, :] * _dot(gn_ref[0:rows, :],
                                                                   wout_ref[...])
        else:
            y_ref[c0:c0 + rows, :] = gn_ref[0:rows, :]

    ext_ref[0:CONV_B_PAD * nb, :] = ext_ref[rows_in * nb:(rows_in + CONV_B_PAD) * nb, :]

    @pl.when(l == n_l - 1)
    def _():
        for g in range(N_GROUPS):
            nhst_ref[g] = ht_ref[g].T


def _ssd_call(xbc, z, dt, cst, hst, consts, *, tm, gate_b=None, wout=None):
    bsz, seq, _ = xbc.shape
    rows_in = min(tm, seq)
    tm_k = max(tm, SSD_CHUNK) if seq < SSD_CHUNK else tm
    n_l = seq // rows_in
    project = wout is not None
    tile = lambda w_: pl.BlockSpec((None, rows_in, w_), lambda i, j: (i, j, 0))
    cstate = pl.BlockSpec((None, CONV_B_PAD, D_XBC), lambda i, j: (i, 0, 0))
    hstate = pl.BlockSpec((None, N_GROUPS, GROUP_WIDTH, D_STATE), lambda i, j: (i, 0, 0, 0))
    args = [xbc, z, dt, cst, hst]
    specs = [tile(D_XBC), tile(D_INNER), tile(LANES), cstate, hstate]
    if project:
        args.append(gate_b)
        specs.append(tile(D_MODEL))
    args += list(consts)
    specs += [_whole()] * len(consts)
    if project:
        args.append(wout)
        specs.append(_whole())
    y_width, y_dtype = (D_MODEL, F32) if project else (D_INNER, BF16)
    return pl.pallas_call(
        functools.partial(_ssd_kernel, tm=tm_k, rows_in=rows_in, n_l=n_l, project=project),
        grid=(bsz, n_l),
        in_specs=specs,
        out_specs=[tile(y_width), cstate, hstate],
        out_shape=[jax.ShapeDtypeStruct((bsz, seq, y_width), y_dtype),
                   jax.ShapeDtypeStruct((bsz, CONV_B_PAD, D_XBC), F32),
                   jax.ShapeDtypeStruct((bsz, N_GROUPS, GROUP_WIDTH, D_STATE), F32)],
        scratch_shapes=[pltpu.VMEM(((tm_k + CONV_B_PAD) * (D_XBC // LANES), LANES), F32),
                        pltpu.VMEM((tm_k * (D_XBC // LANES), LANES), F32),
                        pltpu.VMEM((SSD_CHUNK, D_XBC), F32),
                        pltpu.VMEM((N_GROUPS, D_STATE, GROUP_WIDTH), F32),
                        pltpu.VMEM((SSD_CHUNK, D_INNER), BF16)],
        compiler_params=pltpu.CompilerParams(
            dimension_semantics=("arbitrary", "arbitrary"), vmem_limit_bytes=VMEM_LIMIT),
        name="ssd",
    )(*args)


def _pad_rows_front(a, rows):
    pad = rows - a.shape[-2]
    return jnp.pad(a, [(0, 0)] * (a.ndim - 2) + [(pad, 0), (0, 0)])


def _prepare_params(p):
    w_in = p["w_in"][0]
    s2 = 2 * D_CONV + D_INNER + D_XBC
    s3 = s2 + N_HEADS
    row = lambda v: v.reshape(1, -1).astype(F32)
    lane_pad = lambda v: jnp.pad(jnp.tile(v, (1, HEAD_COPIES)),
                                 ((0, 0), (0, LANES - HEAD_COPIES * v.shape[1])))
    lane_group = np.arange(LANES)[:, None]
    expand = ((lane_group % N_HEADS == np.arange(D_INNER)[None, :] // HEAD_DIM)
              & (lane_group < HEAD_COPIES * N_HEADS))
    q = {}
    for name in ("ffn1", "ffn2"):
        q[name] = (row(p["norm_" + name][0]), p[name + "_w_gate"][0].astype(BF16),
                   p[name + "_w_up"][0].astype(BF16), p[name + "_w_down"][0].astype(BF16))
    w_t = w_in.T
    w_proj = jnp.concatenate([w_t[:s2], w_t[s3:], lane_pad(w_t[s2:s3].T).T], axis=0)
    q["proj"] = (row(p["norm_mix"][0]), w_proj.astype(BF16), lane_pad(row(p["dt_bias"][0])))
    q["conv_a"] = (p["conv_a_w"][0].reshape(-1, LANES), p["conv_a_b"][0].reshape(-1, LANES),
                   row(p["ln_conv_g"][0]), row(p["ln_conv_b"][0]))
    q["w_pw"] = p["w_conv_a_out"][0].astype(BF16)
    q["ssd"] = (p["conv_b_w"][0].reshape(-1, LANES), p["conv_b_b"][0].reshape(-1, LANES),
                lane_pad(row(p["a_log"][0])),
                row(jnp.repeat(p["d_skip"][0], HEAD_DIM)), row(p["ssm_norm_g"][0]),
                jnp.asarray(expand, BF16))
    q["w_ssm_out"] = p["w_ssm_out"][0].astype(BF16)
    q["w_out"] = p["w_out"][0].astype(BF16)
    q["norm_final"] = row(p["norm_final"])
    return q


def _encoder(x, st_a, st_b, st_h, q, *, tm_ffn, tm_mix):
    bsz, seq, _ = x.shape
    t = bsz * seq
    x0 = x.reshape(t, D_MODEL)
    x1 = _ffn_call(x0, *q["ffn1"], tm=min(tm_ffn, t))
    tm_seq = min(tm_mix, seq)
    shp = lambda a: a.reshape(bsz, seq, a.shape[-1])
    st_a = _pad_rows_front(st_a, CONV_A_PAD)
    st_b = _pad_rows_front(st_b, CONV_B_PAD)
    st_h = st_h.reshape(bsz, N_GROUPS, GROUP_WIDTH, D_STATE)
    if seq >= tm_mix:
        m_a, new_a, z, xbc, dt, g_b = _proj_conv_call(
            shp(x1), st_a, q["proj"], q["conv_a"] + (q["w_pw"],), tm=tm_seq)
        m_b, new_b, new_h = _ssd_call(xbc, z, dt, st_b, st_h, q["ssd"], tm=tm_seq,
                                      gate_b=g_b, wout=q["w_ssm_out"])
    else:
        u, z, xbc, dt, g_a, g_b = _proj_call(x1, *q["proj"], tm=min(tm_mix, t))
        act_a, new_a = _conv_a_call(shp(u), st_a, *q["conv_a"], tm=tm_seq)
        m_a = _gated_matmul_call(act_a.reshape(t, D_CONV), q["w_pw"], g_a)
        act_b, new_b, new_h = _ssd_call(shp(xbc), shp(z), shp(dt), st_b, st_h, q["ssd"],
                                        tm=tm_seq)
        m_b = _gated_matmul_call(act_b.reshape(t, D_INNER), q["w_ssm_out"], g_b)
    y = _ffn_call(x1, *q["ffn2"], tm=min(tm_ffn, t),
                  merge=(m_a.reshape(t, D_MODEL), m_b.reshape(t, D_MODEL), q["w_out"]),
                  final_g=q["norm_final"])
    return (y.reshape(bsz, seq, D_MODEL),
            new_a[None, :, CONV_A_PAD - (CONV_A_WIDTH - 1):],
            new_b[None, :, CONV_B_PAD - (CONV_B_WIDTH - 1):],
            new_h.reshape(1, bsz, N_HEADS, HEAD_DIM, D_STATE))


def kernel(x_prompt, x_sample, state_conv_a, state_conv_b, state_ssm, norm_ffn1, ffn1_w_gate, ffn1_w_up, ffn1_w_down, norm_mix, w_in, conv_a_w, conv_a_b, ln_conv_g, ln_conv_b, w_conv_a_out, conv_b_w, conv_b_b, dt_bias, a_log, d_skip, ssm_norm_g, w_ssm_out, w_out, norm_ffn2, ffn2_w_gate, ffn2_w_up, ffn2_w_down, norm_final):
    p = dict(norm_ffn1=norm_ffn1, ffn1_w_gate=ffn1_w_gate, ffn1_w_up=ffn1_w_up,
             ffn1_w_down=ffn1_w_down, norm_mix=norm_mix, w_in=w_in, conv_a_w=conv_a_w,
             conv_a_b=conv_a_b, ln_conv_g=ln_conv_g, ln_conv_b=ln_conv_b,
             w_conv_a_out=w_conv_a_out, conv_b_w=conv_b_w, conv_b_b=conv_b_b, dt_bias=dt_bias,
             a_log=a_log, d_skip=d_skip, ssm_norm_g=ssm_norm_g, w_ssm_out=w_ssm_out, w_out=w_out,
             norm_ffn2=norm_ffn2, ffn2_w_gate=ffn2_w_gate, ffn2_w_up=ffn2_w_up,
             ffn2_w_down=ffn2_w_down, norm_final=norm_final)
    q = _prepare_params(p)
    bp = x_prompt.shape[0]
    zero_a = jnp.zeros((bp, CONV_A_WIDTH - 1, D_CONV), F32)
    zero_b = jnp.zeros((bp, CONV_B_WIDTH - 1, D_XBC), F32)
    zero_h = jnp.zeros((bp, N_HEADS, HEAD_DIM, D_STATE), F32)
    y_p, a_p, b_p, h_p = _encoder(x_prompt, zero_a, zero_b, zero_h, q, tm_ffn=512, tm_mix=256)
    y_s, a_s, b_s, h_s = _encoder(x_sample, state_conv_a[0], state_conv_b[0], state_ssm[0], q,
                                  tm_ffn=256, tm_mix=256)
    return (y_p, y_s, a_p, b_p, h_p, a_s, b_s, h_s)
```

```python
import functools

import jax
import jax.numpy as jnp
import numpy as np
from jax import lax
from jax.experimental import pallas as pl
from jax.experimental.pallas import tpu as pltpu

D_MODEL = 1024
D_FF = 2816
D_CONV = D_MODEL
CONV_A_WIDTH = 31
D_INNER = 2048
HEAD_DIM = 64
N_HEADS = D_INNER // HEAD_DIM
N_GROUPS = 4
HEADS_PER_GROUP = N_HEADS // N_GROUPS
GROUP_WIDTH = D_INNER // N_GROUPS
D_STATE = 128
CONV_B_WIDTH = 4
D_XBC = D_INNER + 2 * N_GROUPS * D_STATE
RMS_EPS = 1e-6
LN_EPS = 1e-5
FFN_RES_WEIGHT = 0.5

LANES = 128
SUBLANES = 8
CONV_A_PAD = 32
CONV_B_PAD = 8
SSD_CHUNK = 128
HEAD_COPIES = 3
FF_CHUNK = 256
PROJ_PANEL = 256
CONV_A_STEPS = 16
CONV_A_FINISH_ROWS = 256
VMEM_LIMIT = 56 * 1024 * 1024
NEG_BIG = -1e30
LOG2_E = 1.4426950408889634

F32 = jnp.float32
BF16 = jnp.bfloat16


def _dot(a, b):
    return jnp.dot(a, b, preferred_element_type=F32)


def _rms_norm(x, g):
    ms = jnp.mean(x * x, axis=-1, keepdims=True)
    return x * lax.rsqrt(ms + RMS_EPS) * g


def _whole():
    return pl.BlockSpec(memory_space=pltpu.VMEM)


def _ffn_kernel(*refs, merge, final):
    it = iter(refs)
    x_ref = next(it)
    if merge:
        ma_ref, yb_ref, gb_ref, wo_ref = next(it), next(it), next(it), next(it)
    ng_ref, wg_ref, wu_ref, wd_ref = next(it), next(it), next(it), next(it)
    if final:
        fg_ref = next(it)
    o_ref = next(it)
    a_scr = next(it)

    x = x_ref[...]
    if merge:
        merged = (ma_ref[...] + gb_ref[...] * yb_ref[...]).astype(BF16)
        x = x + _dot(merged, wo_ref[...])
    h = _rms_norm(x, ng_ref[...]).astype(BF16)
    for c in range(0, D_FF, FF_CHUNK):
        g = _dot(h, wg_ref[:, c:c + FF_CHUNK])
        u = _dot(h, wu_ref[:, c:c + FF_CHUNK])
        a_scr[:, c:c + FF_CHUNK] = (g * jax.nn.sigmoid(g) * u).astype(BF16)
    y = x + FFN_RES_WEIGHT * _dot(a_scr[...], wd_ref[...])
    if final:
        y = _rms_norm(y, fg_ref[...])
    o_ref[...] = y


def _ffn_call(x, ng, wg, wu, wd, *, tm, merge=None, final_g=None):
    t = x.shape[0]
    row = lambda w: pl.BlockSpec((tm, w), lambda i: (i, 0))
    args, specs = [x], [row(D_MODEL)]
    if merge is not None:
        ma, yb, gb, wo = merge
        args += [ma, yb, gb, wo]
        specs += [row(D_MODEL)] * 3 + [_whole()]
    args += [ng, wg, wu, wd]
    specs += [_whole()] * 4
    if final_g is not None:
        args.append(final_g)
        specs.append(_whole())
    return pl.pallas_call(
        functools.partial(_ffn_kernel, merge=merge is not None, final=final_g is not None),
        grid=(t // tm,),
        in_specs=specs,
        out_specs=row(D_MODEL),
        out_shape=jax.ShapeDtypeStruct((t, D_MODEL), F32),
        scratch_shapes=[pltpu.VMEM((tm, D_FF), BF16)],
        compiler_params=pltpu.CompilerParams(
            dimension_semantics=("arbitrary",), vmem_limit_bytes=VMEM_LIMIT),
        name="ffn_merge" if merge is not None else "ffn",
    )(*args)


PROJ_GLU = (0, 2 * D_CONV)
PROJ_Z = (PROJ_GLU[1], PROJ_GLU[1] + D_INNER)
PROJ_XBC = (PROJ_Z[1], PROJ_Z[1] + D_XBC)
PROJ_GATE = (PROJ_XBC[1], PROJ_XBC[1] + 2 * D_MODEL)
PROJ_DT = (PROJ_GATE[1], PROJ_GATE[1] + LANES)


def _project(h, w_ref, seg):
    return _dot(h, w_ref[:, seg[0]:seg[1]])


def _proj_kernel(x_ref, ng_ref, w_ref, dtb_ref, u_ref, z_ref, xbc_ref, dt_ref, ga_ref, gb_ref):
    h = _rms_norm(x_ref[...], ng_ref[...]).astype(BF16)
    glu = _project(h, w_ref, PROJ_GLU)
    u_ref[...] = glu[:, :D_CONV] * jax.nn.sigmoid(glu[:, D_CONV:])
    z_ref[...] = _project(h, w_ref, PROJ_Z)
    xbc_ref[...] = _project(h, w_ref, PROJ_XBC)
    dt_ref[...] = jax.nn.softplus(_project(h, w_ref, PROJ_DT) + dtb_ref[...])
    gates = jax.nn.sigmoid(_project(h, w_ref, PROJ_GATE))
    ga_ref[...] = gates[:, :D_MODEL]
    gb_ref[...] = gates[:, D_MODEL:]


def _proj_call(x, ng, w, dtb, *, tm):
    t = x.shape[0]
    row = lambda w_: pl.BlockSpec((tm, w_), lambda i: (i, 0))
    widths = (D_CONV, D_INNER, D_XBC, LANES, D_MODEL, D_MODEL)
    return pl.pallas_call(
        _proj_kernel,
        grid=(t // tm,),
        in_specs=[row(D_MODEL)] + [_whole()] * 3,
        out_specs=[row(w_) for w_ in widths],
        out_shape=[jax.ShapeDtypeStruct((t, w_), F32) for w_ in widths],
        compiler_params=pltpu.CompilerParams(
            dimension_semantics=("arbitrary",), vmem_limit_bytes=VMEM_LIMIT),
        name="in_proj",
    )(x, ng, w, dtb)


def _store_time_major(dst_ref, t0, src, n, nb):
    for j in range(nb):
        dst_ref[pl.ds(t0 * nb + j, n, stride=nb), :] = src[:, j * LANES:(j + 1) * LANES]


def _load_time_major(dst_ref, src_ref, t0, n, nb):
    for j in range(nb):
        dst_ref[:, j * LANES:(j + 1) * LANES] = src_ref[pl.ds(t0 * nb + j, n, stride=nb), :]


def _conv_time_major(ext_ref, w_ref, b_ref, out_ref, *, n, nb, taps, lag0, steps, act=None,
                     start=0):
    steps = min(steps, n)
    rep = lambda tile: jnp.concatenate([tile] * steps, axis=0)
    for t0 in range(start, start + n, steps):
        acc = rep(b_ref[...])
        for k in range(taps):
            lo = (t0 + lag0 + k) * nb
            acc = acc + rep(w_ref[k * nb:(k + 1) * nb, :]) * ext_ref[lo:lo + steps * nb, :]
        out_ref[t0 * nb:(t0 + steps) * nb, :] = acc if act is None else act(acc)


def _silu(x):
    return x * jax.nn.sigmoid(x)


def _zero_after(x):
    bits = pltpu.bitcast(x[0:SUBLANES, 0:LANES], jnp.int32)
    return lax.shift_right_logical(lax.shift_right_logical(bits, 16), 16).astype(F32)


def _layer_norm_swish(v, g, b):
    mu = jnp.mean(v, axis=-1, keepdims=True)
    vc = v - mu
    var = jnp.mean(vc * vc, axis=-1, keepdims=True)
    return _silu(vc * lax.rsqrt(var + LN_EPS) * g + b)


def _conv_a_kernel(u_ref, st_ref, ga_ref, w_ref, b_ref, lng_ref, lnb_ref, wpw_ref,
                   ma_ref, nst_ref, ext_ref, out_ref, v_ref, *, tm):
    l = pl.program_id(1)
    nb = D_CONV // LANES

    @pl.when(l == 0)
    def _():
        _store_time_major(ext_ref, 0, st_ref, CONV_A_PAD, nb)

    _store_time_major(ext_ref, CONV_A_PAD, u_ref, tm, nb)
    _conv_time_major(ext_ref, w_ref, b_ref, out_ref, n=tm, nb=nb, taps=CONV_A_WIDTH,
                     lag0=CONV_A_PAD - (CONV_A_WIDTH - 1), steps=CONV_A_STEPS)
    _load_time_major(nst_ref, ext_ref, tm, CONV_A_PAD, nb)
    ext_ref[0:CONV_A_PAD * nb, :] = ext_ref[tm * nb:(tm + CONV_A_PAD) * nb, :]
    _load_time_major(v_ref, out_ref, 0, tm, nb)

    swished = _layer_norm_swish(v_ref[...], lng_ref[...], lnb_ref[...]).astype(BF16)
    ma_ref[...] = ga_ref[...] * _dot(swished, wpw_ref[...])


def _conv_a_call(u, st, ga, w, b, lng, lnb, wpw, *, tm):
    bsz, seq, _ = u.shape
    tile = lambda w_: pl.BlockSpec((None, tm, w_), lambda i, j: (i, j, 0))
    state = pl.BlockSpec((None, CONV_A_PAD, D_CONV), lambda i, j: (i, 0, 0))
    return pl.pallas_call(
        functools.partial(_conv_a_kernel, tm=tm),
        grid=(bsz, seq // tm),
        in_specs=[tile(D_CONV), state, tile(D_MODEL)] + [_whole()] * 5,
        out_specs=[tile(D_MODEL), state],
        out_shape=[jax.ShapeDtypeStruct((bsz, seq, D_MODEL), F32),
                   jax.ShapeDtypeStruct((bsz, CONV_A_PAD, D_CONV), F32)],
        scratch_shapes=[pltpu.VMEM(((tm + CONV_A_PAD) * (D_CONV // LANES), LANES), F32),
                        pltpu.VMEM((tm * (D_CONV // LANES), LANES), F32),
                        pltpu.VMEM((tm, D_CONV), F32)],
        compiler_params=pltpu.CompilerParams(
            dimension_semantics=("arbitrary", "arbitrary"), vmem_limit_bytes=VMEM_LIMIT),
        name="conv_a",
    )(u, st, ga, w, b, lng, lnb, wpw)


def _proj_conv_kernel(x_ref, st_ref, ng_ref, w_ref, dtb_ref, cw_ref, cb_ref, lng_ref, lnb_ref,
                      wpw_ref,
                      ma_ref, nst_ref, z_ref, xbc_ref, dt_ref, gb_ref,
                      ext_ref, out_ref, v_ref, h_ref, ga_ref, *, tm):
    l = pl.program_id(1)
    nb = D_CONV // LANES

    @pl.when(l == 0)
    def _():
        _store_time_major(ext_ref, 0, st_ref, CONV_A_PAD, nb)

    h = _rms_norm(x_ref[...], ng_ref[...]).astype(BF16)
    glu = _project(h, w_ref, PROJ_GLU)
    _store_time_major(ext_ref, CONV_A_PAD, glu[:, :D_CONV] * jax.nn.sigmoid(glu[:, D_CONV:]),
                      tm, nb)
    dt_ref[...] = jax.nn.softplus(_project(h, w_ref, PROJ_DT) + dtb_ref[...])
    h_ref[...] = h
    width = PROJ_PANEL
    panels = ([(ga_ref, PROJ_GATE[0] + c, c, jax.nn.sigmoid)
               for c in range(0, D_MODEL, width)]
              + [(z_ref, PROJ_Z[0] + c, c, None) for c in range(0, D_INNER, width)]
              + [(xbc_ref, PROJ_XBC[0] + c, c, None) for c in range(0, D_XBC, width)]
              + [(gb_ref, PROJ_GATE[0] + D_MODEL + c, c, jax.nn.sigmoid)
                 for c in range(0, D_MODEL, width)])
    steps = CONV_A_STEPS
    n_blocks = tm // steps
    finish_every = CONV_A_FINISH_ROWS // steps
    done = 0
    tail = jnp.zeros((SUBLANES, LANES), F32)
    for i in range(n_blocks):
        want = (len(panels) * (i + 1)) // n_blocks
        for dst, wc, oc, act in panels[done:want]:
            r = _project(h_ref[...], w_ref, (wc, wc + width))
            dst[:, oc:oc + width] = r if act is None else act(r)
            tail = tail + _zero_after(r)
        done = want
        block_tail = jnp.concatenate([tail] * (steps * nb // SUBLANES), axis=0)
        _conv_time_major(ext_ref, cw_ref, cb_ref, out_ref, start=i * steps, n=steps, nb=nb,
                         taps=CONV_A_WIDTH, lag0=CONV_A_PAD - (CONV_A_WIDTH - 1), steps=steps,
                         act=lambda a: a + block_tail)
        tail = jnp.zeros((SUBLANES, LANES), F32)
        if (i + 1) % finish_every == 0:
            r0 = (i + 1 - finish_every) * steps
            rows = slice(r0, r0 + CONV_A_FINISH_ROWS)
            for j in range(nb):
                v_ref[rows, j * LANES:(j + 1) * LANES] = out_ref[
                    pl.ds(r0 * nb + j, CONV_A_FINISH_ROWS, stride=nb), :]
            swished = _layer_norm_swish(v_ref[rows, :], lng_ref[...], lnb_ref[...]).astype(BF16)
            y_a = _dot(swished, wpw_ref[...])
            ma_ref[rows, :] = ga_ref[rows, :] * y_a
            tail = _zero_after(y_a)
    _load_time_major(nst_ref, ext_ref, tm, CONV_A_PAD, nb)
    ext_ref[0:CONV_A_PAD * nb, :] = ext_ref[tm * nb:(tm + CONV_A_PAD) * nb, :]


def _proj_conv_call(x, st, proj_w, conv_w, *, tm):
    bsz, seq, _ = x.shape
    nb = D_CONV // LANES
    tile = lambda w_: pl.BlockSpec((None, tm, w_), lambda i, j: (i, j, 0))
    state = pl.BlockSpec((None, CONV_A_PAD, D_CONV), lambda i, j: (i, 0, 0))
    widths = (D_MODEL, D_INNER, D_XBC, LANES, D_MODEL)
    outs = [jax.ShapeDtypeStruct((bsz, seq, w), F32) for w in widths]
    outs.insert(1, jax.ShapeDtypeStruct((bsz, CONV_A_PAD, D_CONV), F32))
    out_specs = [tile(w) for w in widths]
    out_specs.insert(1, state)
    return pl.pallas_call(
        functools.partial(_proj_conv_kernel, tm=tm),
        grid=(bsz, seq // tm),
        in_specs=[tile(D_MODEL), state] + [_whole()] * (len(proj_w) + len(conv_w)),
        out_specs=out_specs,
        out_shape=outs,
        scratch_shapes=[pltpu.VMEM(((tm + CONV_A_PAD) * nb, LANES), F32),
                        pltpu.VMEM((tm * nb, LANES), F32),
                        pltpu.VMEM((tm, D_CONV), F32),
                        pltpu.VMEM((tm, D_MODEL), BF16),
                        pltpu.VMEM((tm, D_MODEL), F32)],
        compiler_params=pltpu.CompilerParams(
            dimension_semantics=("arbitrary", "arbitrary"), vmem_limit_bytes=VMEM_LIMIT),
        name="proj_conv",
    )(x, st, *proj_w, *conv_w)


def _cumsum_rows(a):
    q = a.shape[0]
    row = lax.broadcasted_iota(jnp.int32, a.shape, 0)
    s = 1
    while s < q:
        a = a + jnp.where(row >= s, pltpu.roll(a, s, axis=0), 0.0)
        s *= 2
    return a


def _split_terms(v):
    lane = lax.broadcasted_iota(jnp.int32, v.shape, 1)
    hi = v.astype(BF16)
    rest = v - hi.astype(F32)
    mid = rest.astype(BF16)
    lo = (rest - mid.astype(F32)).astype(BF16)
    return jnp.where(lane < N_HEADS, hi, jnp.where(lane < 2 * N_HEADS, mid, lo))


def _ssd_kernel(xbc_ref, z_ref, dt_ref, cst_ref, hst_ref, w_ref, b_ref, alog_ref, dskip_ref,
                ng_ref, expand_ref, wout_ref,
                y_ref, ncst_ref, nhst_ref,
                ext_ref, cout_ref, xc_ref, ht_ref, gn_ref, *, tm, rows_in, n_l):
    l = pl.program_id(1)
    q = SSD_CHUNK
    nb = D_XBC // LANES

    @pl.when(l == 0)
    def _():
        _store_time_major(ext_ref, 0, cst_ref, CONV_B_PAD, nb)
        for g in range(N_GROUPS):
            ht_ref[g] = hst_ref[g].T

    if rows_in < tm:
        ext_ref[(CONV_B_PAD + rows_in) * nb:(CONV_B_PAD + tm) * nb, :] = jnp.zeros(
            ((tm - rows_in) * nb, LANES), F32)
    _store_time_major(ext_ref, CONV_B_PAD, xbc_ref, rows_in, nb)
    _load_time_major(ncst_ref, ext_ref, rows_in, CONV_B_PAD, nb)

    a_row = -jnp.exp(alog_ref[...])
    head_ok = lax.broadcasted_iota(jnp.int32, (1, LANES), 1) < HEAD_COPIES * N_HEADS
    a_row = jnp.where(head_ok, a_row, 0.0)
    causal = (lax.broadcasted_iota(jnp.int32, (q, q), 0)
              >= lax.broadcasted_iota(jnp.int32, (q, q), 1))
    not_causal = jnp.where(causal, 0.0, NEG_BIG)
    lane_lo = lax.broadcasted_iota(jnp.int32, (q, LANES), 1) < HEAD_DIM
    zero_b = jnp.zeros((q, LANES), BF16)

    for c0 in range(0, tm, q):
        rows = min(q, rows_in - c0)
        _conv_time_major(ext_ref, w_ref, b_ref, cout_ref, start=c0, n=q, nb=nb, taps=CONV_B_WIDTH,
                         lag0=CONV_B_PAD - (CONV_B_WIDTH - 1), steps=8, act=_silu)
        _load_time_major(xc_ref, cout_ref, c0, q, nb)

        if rows == q:
            dt = dt_ref[c0:c0 + q, :]
        else:
            dt = jnp.concatenate([dt_ref[c0:c0 + rows, :], jnp.zeros((q - rows, LANES), F32)],
                                 axis=0)
        acum = _cumsum_rows(dt * a_row)
        a_last = acum[q - 1:q, :]
        e_acum = jnp.exp(acum)
        s_tail = dt * jnp.exp(a_last - acum)
        acum2 = acum * LOG2_E
        col2 = acum2
        row2_t = (acum2 - jnp.log2(dt)).T

        terms = _split_terms(jnp.concatenate([e_acum, s_tail], axis=0))

        for g in range(N_GROUPS):
            gs = slice(g * GROUP_WIDTH, (g + 1) * GROUP_WIDTH)
            expanded = _dot(terms, expand_ref[:, gs])
            e_acum_x = expanded[0:q]
            s_tail_x = expanded[q:2 * q]
            e_last_x = expanded[q - 1:q]
            xs = xc_ref[:, gs]
            xs_b = xs.astype(BF16)
            b_g = xc_ref[:, D_INNER + g * D_STATE:D_INNER + (g + 1) * D_STATE]
            c_g = xc_ref[:, D_INNER + (N_GROUPS + g) * D_STATE:
                         D_INNER + (N_GROUPS + g + 1) * D_STATE]
            c_gb = c_g.astype(BF16)
            cb = lax.dot_general(c_gb, b_g.astype(BF16), (((1,), (1,)), ((), ())),
                                 preferred_element_type=F32)
            ht = ht_ref[g]
            y_st = _dot(c_gb, ht.astype(BF16)) * e_acum_x
            y_in = []
            for pr in range(HEADS_PER_GROUP // 2):
                ms = []
                for hh in range(2):
                    hd = g * HEADS_PER_GROUP + 2 * pr + hh
                    seg2 = col2[:, hd:hd + 1] - row2_t[hd:hd + 1, :] + not_causal
                    ms.append((cb * jnp.exp2(seg2)).astype(BF16))
                lhs = jnp.concatenate(ms, axis=1)
                xp = xs_b[:, pr * LANES:(pr + 1) * LANES]
                rhs = jnp.concatenate([jnp.where(lane_lo, xp, zero_b),
                                       jnp.where(lane_lo, zero_b, xp)], axis=0)
                y_in.append(_dot(lhs, rhs))
            ht_ref[g] = e_last_x * ht + _dot(b_g.T.astype(BF16), (xs * s_tail_x).astype(BF16))

            y = jnp.concatenate(y_in, axis=1) + y_st
            gated = ((y + dskip_ref[:, gs] * xs)[0:rows] * _silu(z_ref[c0:c0 + rows, gs]))
            ms = jnp.mean(gated * gated, axis=-1, keepdims=True)
            gn_ref[0:rows, gs] = (gated * lax.rsqrt(ms + RMS_EPS) * ng_ref[:, gs]).astype(BF16)

        y_ref[c0:c0 + rows, :] = _dot(gn_ref[0:rows, :], wout_ref[...])

    ext_ref[0:CONV_B_PAD * nb, :] = ext_ref[rows_in * nb:(rows_in + CONV_B_PAD) * nb, :]

    @pl.when(l == n_l - 1)
    def _():
        for g in range(N_GROUPS):
            nhst_ref[g] = ht_ref[g].T


def _ssd_call(xbc, z, dt, cst, hst, consts, *, tm):
    bsz, seq, _ = xbc.shape
    rows_in = min(tm, seq)
    tm_k = max(tm, SSD_CHUNK) if seq < SSD_CHUNK else tm
    n_l = seq // rows_in
    tile = lambda w_: pl.BlockSpec((None, rows_in, w_), lambda i, j: (i, j, 0))
    cstate = pl.BlockSpec((None, CONV_B_PAD, D_XBC), lambda i, j: (i, 0, 0))
    hstate = pl.BlockSpec((None, N_GROUPS, GROUP_WIDTH, D_STATE), lambda i, j: (i, 0, 0, 0))
    return pl.pallas_call(
        functools.partial(_ssd_kernel, tm=tm_k, rows_in=rows_in, n_l=n_l),
        grid=(bsz, n_l),
        in_specs=[tile(D_XBC), tile(D_INNER), tile(LANES), cstate, hstate]
        + [_whole()] * len(consts),
        out_specs=[tile(D_MODEL), cstate, hstate],
        out_shape=[jax.ShapeDtypeStruct((bsz, seq, D_MODEL), F32),
                   jax.ShapeDtypeStruct((bsz, CONV_B_PAD, D_XBC), F32),
                   jax.ShapeDtypeStruct((bsz, N_GROUPS, GROUP_WIDTH, D_STATE), F32)],
        scratch_shapes=[pltpu.VMEM(((tm_k + CONV_B_PAD) * (D_XBC // LANES), LANES), F32),
                        pltpu.VMEM((tm_k * (D_XBC // LANES), LANES), F32),
                        pltpu.VMEM((SSD_CHUNK, D_XBC), F32),
                        pltpu.VMEM((N_GROUPS, D_STATE, GROUP_WIDTH), F32),
                        pltpu.VMEM((SSD_CHUNK, D_INNER), BF16)],
        compiler_params=pltpu.CompilerParams(
            dimension_semantics=("arbitrary", "arbitrary"), vmem_limit_bytes=VMEM_LIMIT),
        name="ssd",
    )(xbc, z, dt, cst, hst, *consts)


def _pad_rows_front(a, rows):
    pad = rows - a.shape[-2]
    return jnp.pad(a, [(0, 0)] * (a.ndim - 2) + [(pad, 0), (0, 0)])


def _prepare_params(p):
    w_in = p["w_in"][0]
    s2 = 2 * D_CONV + D_INNER + D_XBC
    s3 = s2 + N_HEADS
    row = lambda v: v.reshape(1, -1).astype(F32)
    lane_pad = lambda v: jnp.pad(jnp.tile(v, (1, HEAD_COPIES)),
                                 ((0, 0), (0, LANES - HEAD_COPIES * v.shape[1])))
    lane_group = np.arange(LANES)[:, None]
    expand = ((lane_group % N_HEADS == np.arange(D_INNER)[None, :] // HEAD_DIM)
              & (lane_group < HEAD_COPIES * N_HEADS))
    q = {}
    for name in ("ffn1", "ffn2"):
        q[name] = (row(p["norm_" + name][0]), p[name + "_w_gate"][0].astype(BF16),
                   p[name + "_w_up"][0].astype(BF16), p[name + "_w_down"][0].astype(BF16))
    w_proj = jnp.concatenate([w_in[:, :s2], w_in[:, s3:], lane_pad(w_in[:, s2:s3])], axis=1)
    q["proj"] = (row(p["norm_mix"][0]), w_proj.astype(BF16), lane_pad(row(p["dt_bias"][0])))
    q["conv_a"] = (p["conv_a_w"][0].reshape(-1, LANES), p["conv_a_b"][0].reshape(-1, LANES),
                   row(p["ln_conv_g"][0]), row(p["ln_conv_b"][0]),
                   p["w_conv_a_out"][0].astype(BF16))
    q["ssd"] = (p["conv_b_w"][0].reshape(-1, LANES), p["conv_b_b"][0].reshape(-1, LANES),
                lane_pad(row(p["a_log"][0])),
                row(jnp.repeat(p["d_skip"][0], HEAD_DIM)), row(p["ssm_norm_g"][0]),
                jnp.asarray(expand, BF16), p["w_ssm_out"][0].astype(BF16))
    q["w_out"] = p["w_out"][0].astype(BF16)
    q["norm_final"] = row(p["norm_final"])
    return q


def _encoder(x, st_a, st_b, st_h, q, *, tm_ffn, tm_mix):
    bsz, seq, _ = x.shape
    t = bsz * seq
    x0 = x.reshape(t, D_MODEL)
    x1 = _ffn_call(x0, *q["ffn1"], tm=min(tm_ffn, t))
    tm_seq = min(tm_mix, seq)
    shp = lambda a: a.reshape(bsz, seq, a.shape[-1])
    st_a = _pad_rows_front(st_a, CONV_A_PAD)
    st_b = _pad_rows_front(st_b, CONV_B_PAD)
    st_h = st_h.reshape(bsz, N_GROUPS, GROUP_WIDTH, D_STATE)
    if seq >= tm_mix:
        m_a, new_a, z, xbc, dt, g_b = _proj_conv_call(shp(x1), st_a, q["proj"], q["conv_a"],
                                                      tm=tm_seq)
    else:
        u, z, xbc, dt, g_a, g_b = _proj_call(x1, *q["proj"], tm=min(tm_mix, t))
        m_a, new_a = _conv_a_call(shp(u), st_a, shp(g_a), *q["conv_a"], tm=tm_seq)
    y_b, new_b, new_h = _ssd_call(shp(xbc), shp(z), shp(dt), st_b, st_h, q["ssd"], tm=tm_seq)
    flat = lambda a: a.reshape(t, D_MODEL)
    y = _ffn_call(x1, *q["ffn2"], tm=min(tm_ffn, t),
                  merge=(flat(m_a), flat(y_b), flat(g_b), q["w_out"]), final_g=q["norm_final"])
    return (y.reshape(bsz, seq, D_MODEL),
            new_a[None, :, CONV_A_PAD - (CONV_A_WIDTH - 1):],
            new_b[None, :, CONV_B_PAD - (CONV_B_WIDTH - 1):],
            new_h.reshape(1, bsz, N_HEADS, HEAD_DIM, D_STATE))


def kernel(x_prompt, x_sample, state_conv_a, state_conv_b, state_ssm, norm_ffn1, ffn1_w_gate, ffn1_w_up, ffn1_w_down, norm_mix, w_in, conv_a_w, conv_a_b, ln_conv_g, ln_conv_b, w_conv_a_out, conv_b_w, conv_b_b, dt_bias, a_log, d_skip, ssm_norm_g, w_ssm_out, w_out, norm_ffn2, ffn2_w_gate, ffn2_w_up, ffn2_w_down, norm_final):
    p = dict(norm_ffn1=norm_ffn1, ffn1_w_gate=ffn1_w_gate, ffn1_w_up=ffn1_w_up,
             ffn1_w_down=ffn1_w_down, norm_mix=norm_mix, w_in=w_in, conv_a_w=conv_a_w,
             conv_a_b=conv_a_b, ln_conv_g=ln_conv_g, ln_conv_b=ln_conv_b,
             w_conv_a_out=w_conv_a_out, conv_b_w=conv_b_w, conv_b_b=conv_b_b, dt_bias=dt_bias,
             a_log=a_log, d_skip=d_skip, ssm_norm_g=ssm_norm_g, w_ssm_out=w_ssm_out, w_out=w_out,
             norm_ffn2=norm_ffn2, ffn2_w_gate=ffn2_w_gate, ffn2_w_up=ffn2_w_up,
             ffn2_w_down=ffn2_w_down, norm_final=norm_final)
    q = _prepare_params(p)
    bp = x_prompt.shape[0]
    zero_a = jnp.zeros((bp, CONV_A_WIDTH - 1, D_CONV), F32)
    zero_b = jnp.zeros((bp, CONV_B_WIDTH - 1, D_XBC), F32)
    zero_h = jnp.zeros((bp, N_HEADS, HEAD_DIM, D_STATE), F32)
    y_p, a_p, b_p, h_p = _encoder(x_prompt, zero_a, zero_b, zero_h, q, tm_ffn=512, tm_mix=256)
    y_s, a_s, b_s, h_s = _encoder(x_sample, state_conv_a[0], state_conv_b[0], state_ssm[0], q,
                                  tm_ffn=256, tm_mix=256)
    return (y_p, y_s, a_p, b_p, h_p, a_s, b_s, h_s)
```

```python
import functools

import jax
import jax.numpy as jnp
import numpy as np
from jax import lax
from jax.experimental import pallas as pl
from jax.experimental.pallas import tpu as pltpu

D_MODEL = 1024
D_FF = 2816
D_CONV = D_MODEL
CONV_A_WIDTH = 31
D_INNER = 2048
HEAD_DIM = 64
N_HEADS = D_INNER // HEAD_DIM
N_GROUPS = 4
HEADS_PER_GROUP = N_HEADS // N_GROUPS
GROUP_WIDTH = D_INNER // N_GROUPS
D_STATE = 128
CONV_B_WIDTH = 4
D_XBC = D_INNER + 2 * N_GROUPS * D_STATE
RMS_EPS = 1e-6
LN_EPS = 1e-5
FFN_RES_WEIGHT = 0.5

LANES = 128
SUBLANES = 8
CONV_A_PAD = 32
CONV_B_PAD = 8
SSD_CHUNK = 128
HEAD_COPIES = 3
FF_CHUNK = 256
PROJ_PANEL = 256
CONV_A_STEPS = 16
CONV_A_FINISH_ROWS = 256
VMEM_LIMIT = 56 * 1024 * 1024
NEG_BIG = -1e30
LOG2_E = 1.4426950408889634

F32 = jnp.float32
BF16 = jnp.bfloat16


def _dot(a, b):
    return jnp.dot(a, b, preferred_element_type=F32)


def _rms_norm(x, g):
    ms = jnp.mean(x * x, axis=-1, keepdims=True)
    return x * lax.rsqrt(ms + RMS_EPS) * g


def _sigmoid(x):
    return 0.5 * jnp.tanh(0.5 * x) + 0.5


def _silu(x):
    half = 0.5 * x
    return half + half * jnp.tanh(half)


def _whole():
    return pl.BlockSpec(memory_space=pltpu.VMEM)


def _ffn_kernel(*refs, merge, final):
    it = iter(refs)
    x_ref = next(it)
    if merge:
        ma_ref, yb_ref, gb_ref, wo_ref = next(it), next(it), next(it), next(it)
    ng_ref, wg_ref, wu_ref, wd_ref = next(it), next(it), next(it), next(it)
    if final:
        fg_ref = next(it)
    o_ref = next(it)
    a_scr = next(it)

    x = x_ref[...]
    if merge:
        merged = (ma_ref[...] + gb_ref[...] * yb_ref[...]).astype(BF16)
        x = x + _dot(merged, wo_ref[...])
    h = _rms_norm(x, ng_ref[...]).astype(BF16)
    for c in range(0, D_FF, FF_CHUNK):
        g = _dot(h, wg_ref[:, c:c + FF_CHUNK])
        u = _dot(h, wu_ref[:, c:c + FF_CHUNK])
        a_scr[:, c:c + FF_CHUNK] = (_silu(g) * u).astype(BF16)
    y = x + FFN_RES_WEIGHT * _dot(a_scr[...], wd_ref[...])
    if final:
        y = _rms_norm(y, fg_ref[...])
    o_ref[...] = y


def _ffn_call(x, ng, wg, wu, wd, *, tm, merge=None, final_g=None):
    t = x.shape[0]
    row = lambda w: pl.BlockSpec((tm, w), lambda i: (i, 0))
    args, specs = [x], [row(D_MODEL)]
    if merge is not None:
        ma, yb, gb, wo = merge
        args += [ma, yb, gb, wo]
        specs += [row(D_MODEL)] * 3 + [_whole()]
    args += [ng, wg, wu, wd]
    specs += [_whole()] * 4
    if final_g is not None:
        args.append(final_g)
        specs.append(_whole())
    return pl.pallas_call(
        functools.partial(_ffn_kernel, merge=merge is not None, final=final_g is not None),
        grid=(t // tm,),
        in_specs=specs,
        out_specs=row(D_MODEL),
        out_shape=jax.ShapeDtypeStruct((t, D_MODEL), F32),
        scratch_shapes=[pltpu.VMEM((tm, D_FF), BF16)],
        compiler_params=pltpu.CompilerParams(
            dimension_semantics=("arbitrary",), vmem_limit_bytes=VMEM_LIMIT),
        name="ffn_merge" if merge is not None else "ffn",
    )(*args)


PROJ_GLU = (0, 2 * D_CONV)
PROJ_Z = (PROJ_GLU[1], PROJ_GLU[1] + D_INNER)
PROJ_XBC = (PROJ_Z[1], PROJ_Z[1] + D_XBC)
PROJ_GATE = (PROJ_XBC[1], PROJ_XBC[1] + 2 * D_MODEL)
PROJ_DT = (PROJ_GATE[1], PROJ_GATE[1] + LANES)


def _project(h, w_ref, seg):
    return _dot(h, w_ref[:, seg[0]:seg[1]])


def _proj_kernel(x_ref, ng_ref, w_ref, dtb_ref, u_ref, z_ref, xbc_ref, dt_ref, ga_ref, gb_ref):
    h = _rms_norm(x_ref[...], ng_ref[...]).astype(BF16)
    glu = _project(h, w_ref, PROJ_GLU)
    u_ref[...] = glu[:, :D_CONV] * _sigmoid(glu[:, D_CONV:])
    z_ref[...] = _project(h, w_ref, PROJ_Z)
    xbc_ref[...] = _project(h, w_ref, PROJ_XBC)
    dt_ref[...] = jax.nn.softplus(_project(h, w_ref, PROJ_DT) + dtb_ref[...])
    gates = _sigmoid(_project(h, w_ref, PROJ_GATE))
    ga_ref[...] = gates[:, :D_MODEL]
    gb_ref[...] = gates[:, D_MODEL:]


def _proj_call(x, ng, w, dtb, *, tm):
    t = x.shape[0]
    row = lambda w_: pl.BlockSpec((tm, w_), lambda i: (i, 0))
    widths = (D_CONV, D_INNER, D_XBC, LANES, D_MODEL, D_MODEL)
    return pl.pallas_call(
        _proj_kernel,
        grid=(t // tm,),
        in_specs=[row(D_MODEL)] + [_whole()] * 3,
        out_specs=[row(w_) for w_ in widths],
        out_shape=[jax.ShapeDtypeStruct((t, w_), F32) for w_ in widths],
        compiler_params=pltpu.CompilerParams(
            dimension_semantics=("arbitrary",), vmem_limit_bytes=VMEM_LIMIT),
        name="in_proj",
    )(x, ng, w, dtb)


def _store_time_major(dst_ref, t0, src, n, nb):
    for j in range(nb):
        dst_ref[pl.ds(t0 * nb + j, n, stride=nb), :] = src[:, j * LANES:(j + 1) * LANES]


def _load_time_major(dst_ref, src_ref, t0, n, nb):
    for j in range(nb):
        dst_ref[:, j * LANES:(j + 1) * LANES] = src_ref[pl.ds(t0 * nb + j, n, stride=nb), :]


def _conv_time_major(ext_ref, w_ref, b_ref, out_ref, *, n, nb, taps, lag0, steps, act=None,
                     start=0):
    steps = min(steps, n)
    rep = lambda tile: jnp.concatenate([tile] * steps, axis=0)
    for t0 in range(start, start + n, steps):
        acc = rep(b_ref[...])
        for k in range(taps):
            lo = (t0 + lag0 + k) * nb
            acc = acc + rep(w_ref[k * nb:(k + 1) * nb, :]) * ext_ref[lo:lo + steps * nb, :]
        out_ref[t0 * nb:(t0 + steps) * nb, :] = acc if act is None else act(acc)


def _zero_after(x):
    bits = pltpu.bitcast(x[0:SUBLANES, 0:LANES], jnp.int32)
    return lax.shift_right_logical(lax.shift_right_logical(bits, 16), 16).astype(F32)


def _layer_norm_swish(v, g, b):
    mu = jnp.mean(v, axis=-1, keepdims=True)
    vc = v - mu
    var = jnp.mean(vc * vc, axis=-1, keepdims=True)
    return _silu(vc * lax.rsqrt(var + LN_EPS) * g + b)


def _conv_a_kernel(u_ref, st_ref, ga_ref, w_ref, b_ref, lng_ref, lnb_ref, wpw_ref,
                   ma_ref, nst_ref, ext_ref, out_ref, v_ref, *, tm):
    l = pl.program_id(1)
    nb = D_CONV // LANES

    @pl.when(l == 0)
    def _():
        _store_time_major(ext_ref, 0, st_ref, CONV_A_PAD, nb)

    _store_time_major(ext_ref, CONV_A_PAD, u_ref, tm, nb)
    _conv_time_major(ext_ref, w_ref, b_ref, out_ref, n=tm, nb=nb, taps=CONV_A_WIDTH,
                     lag0=CONV_A_PAD - (CONV_A_WIDTH - 1), steps=CONV_A_STEPS)
    _load_time_major(nst_ref, ext_ref, tm, CONV_A_PAD, nb)
    ext_ref[0:CONV_A_PAD * nb, :] = ext_ref[tm * nb:(tm + CONV_A_PAD) * nb, :]
    _load_time_major(v_ref, out_ref, 0, tm, nb)

    swished = _layer_norm_swish(v_ref[...], lng_ref[...], lnb_ref[...]).astype(BF16)
    ma_ref[...] = ga_ref[...] * _dot(swished, wpw_ref[...])


def _conv_a_call(u, st, ga, w, b, lng, lnb, wpw, *, tm):
    bsz, seq, _ = u.shape
    tile = lambda w_: pl.BlockSpec((None, tm, w_), lambda i, j: (i, j, 0))
    state = pl.BlockSpec((None, CONV_A_PAD, D_CONV), lambda i, j: (i, 0, 0))
    return pl.pallas_call(
        functools.partial(_conv_a_kernel, tm=tm),
        grid=(bsz, seq // tm),
        in_specs=[tile(D_CONV), state, tile(D_MODEL)] + [_whole()] * 5,
        out_specs=[tile(D_MODEL), state],
        out_shape=[jax.ShapeDtypeStruct((bsz, seq, D_MODEL), F32),
                   jax.ShapeDtypeStruct((bsz, CONV_A_PAD, D_CONV), F32)],
        scratch_shapes=[pltpu.VMEM(((tm + CONV_A_PAD) * (D_CONV // LANES), LANES), F32),
                        pltpu.VMEM((tm * (D_CONV // LANES), LANES), F32),
                        pltpu.VMEM((tm, D_CONV), F32)],
        compiler_params=pltpu.CompilerParams(
            dimension_semantics=("arbitrary", "arbitrary"), vmem_limit_bytes=VMEM_LIMIT),
        name="conv_a",
    )(u, st, ga, w, b, lng, lnb, wpw)


def _proj_conv_kernel(x_ref, st_ref, ng_ref, w_ref, dtb_ref, cw_ref, cb_ref, lng_ref, lnb_ref,
                      wpw_ref,
                      ma_ref, nst_ref, z_ref, xbc_ref, dt_ref, gb_ref,
                      ext_ref, out_ref, v_ref, h_ref, ga_ref, *, tm):
    l = pl.program_id(1)
    nb = D_CONV // LANES

    @pl.when(l == 0)
    def _():
        _store_time_major(ext_ref, 0, st_ref, CONV_A_PAD, nb)

    h = _rms_norm(x_ref[...], ng_ref[...]).astype(BF16)
    glu = _project(h, w_ref, PROJ_GLU)
    _store_time_major(ext_ref, CONV_A_PAD, glu[:, :D_CONV] * _sigmoid(glu[:, D_CONV:]), tm, nb)
    dt_ref[...] = jax.nn.softplus(_project(h, w_ref, PROJ_DT) + dtb_ref[...])
    h_ref[...] = h
    width = PROJ_PANEL
    panels = ([(ga_ref, PROJ_GATE[0] + c, c, _sigmoid)
               for c in range(0, D_MODEL, width)]
              + [(z_ref, PROJ_Z[0] + c, c, None) for c in range(0, D_INNER, width)]
              + [(xbc_ref, PROJ_XBC[0] + c, c, None) for c in range(0, D_XBC, width)]
              + [(gb_ref, PROJ_GATE[0] + D_MODEL + c, c, _sigmoid)
                 for c in range(0, D_MODEL, width)])
    steps = CONV_A_STEPS
    n_blocks = tm // steps
    finish_every = CONV_A_FINISH_ROWS // steps
    done = 0
    tail = jnp.zeros((SUBLANES, LANES), F32)
    for i in range(n_blocks):
        want = (len(panels) * (i + 1)) // n_blocks
        for dst, wc, oc, act in panels[done:want]:
            r = _project(h_ref[...], w_ref, (wc, wc + width))
            dst[:, oc:oc + width] = r if act is None else act(r)
            tail = tail + _zero_after(r)
        done = want
        block_tail = jnp.concatenate([tail] * (steps * nb // SUBLANES), axis=0)
        _conv_time_major(ext_ref, cw_ref, cb_ref, out_ref, start=i * steps, n=steps, nb=nb,
                         taps=CONV_A_WIDTH, lag0=CONV_A_PAD - (CONV_A_WIDTH - 1), steps=steps,
                         act=lambda a: a + block_tail)
        tail = jnp.zeros((SUBLANES, LANES), F32)
        if (i + 1) % finish_every == 0:
            r0 = (i + 1 - finish_every) * steps
            rows = slice(r0, r0 + CONV_A_FINISH_ROWS)
            for j in range(nb):
                v_ref[rows, j * LANES:(j + 1) * LANES] = out_ref[
                    pl.ds(r0 * nb + j, CONV_A_FINISH_ROWS, stride=nb), :]
            swished = _layer_norm_swish(v_ref[rows, :], lng_ref[...], lnb_ref[...]).astype(BF16)
            y_a = _dot(swished, wpw_ref[...])
            ma_ref[rows, :] = ga_ref[rows, :] * y_a
            tail = _zero_after(y_a)
    _load_time_major(nst_ref, ext_ref, tm, CONV_A_PAD, nb)
    ext_ref[0:CONV_A_PAD * nb, :] = ext_ref[tm * nb:(tm + CONV_A_PAD) * nb, :]


def _proj_conv_call(x, st, proj_w, conv_w, *, tm):
    bsz, seq, _ = x.shape
    nb = D_CONV // LANES
    tile = lambda w_: pl.BlockSpec((None, tm, w_), lambda i, j: (i, j, 0))
    state = pl.BlockSpec((None, CONV_A_PAD, D_CONV), lambda i, j: (i, 0, 0))
    widths = (D_MODEL, D_INNER, D_XBC, LANES, D_MODEL)
    outs = [jax.ShapeDtypeStruct((bsz, seq, w), F32) for w in widths]
    outs.insert(1, jax.ShapeDtypeStruct((bsz, CONV_A_PAD, D_CONV), F32))
    out_specs = [tile(w) for w in widths]
    out_specs.insert(1, state)
    return pl.pallas_call(
        functools.partial(_proj_conv_kernel, tm=tm),
        grid=(bsz, seq // tm),
        in_specs=[tile(D_MODEL), state] + [_whole()] * (len(proj_w) + len(conv_w)),
        out_specs=out_specs,
        out_shape=outs,
        scratch_shapes=[pltpu.VMEM(((tm + CONV_A_PAD) * nb, LANES), F32),
                        pltpu.VMEM((tm * nb, LANES), F32),
                        pltpu.VMEM((tm, D_CONV), F32),
                        pltpu.VMEM((tm, D_MODEL), BF16),
                        pltpu.VMEM((tm, D_MODEL), F32)],
        compiler_params=pltpu.CompilerParams(
            dimension_semantics=("arbitrary", "arbitrary"), vmem_limit_bytes=VMEM_LIMIT),
        name="proj_conv",
    )(x, st, *proj_w, *conv_w)


def _cumsum_rows(a):
    q = a.shape[0]
    row = lax.broadcasted_iota(jnp.int32, a.shape, 0)
    s = 1
    while s < q:
        a = a + jnp.where(row >= s, pltpu.roll(a, s, axis=0), 0.0)
        s *= 2
    return a


def _split_terms(v):
    lane = lax.broadcasted_iota(jnp.int32, v.shape, 1)
    hi = v.astype(BF16)
    rest = v - hi.astype(F32)
    mid = rest.astype(BF16)
    lo = (rest - mid.astype(F32)).astype(BF16)
    return jnp.where(lane < N_HEADS, hi, jnp.where(lane < 2 * N_HEADS, mid, lo))


def _ssd_kernel(xbc_ref, z_ref, dt_ref, cst_ref, hst_ref, w_ref, b_ref, alog_ref, dskip_ref,
                ng_ref, expand_ref, wout_ref,
                y_ref, ncst_ref, nhst_ref,
                ext_ref, cout_ref, xc_ref, ht_ref, gn_ref, *, tm, rows_in, n_l):
    l = pl.program_id(1)
    q = SSD_CHUNK
    nb = D_XBC // LANES

    @pl.when(l == 0)
    def _():
        _store_time_major(ext_ref, 0, cst_ref, CONV_B_PAD, nb)
        for g in range(N_GROUPS):
            ht_ref[g] = hst_ref[g].T

    if rows_in < tm:
        ext_ref[(CONV_B_PAD + rows_in) * nb:(CONV_B_PAD + tm) * nb, :] = jnp.zeros(
            ((tm - rows_in) * nb, LANES), F32)
    _store_time_major(ext_ref, CONV_B_PAD, xbc_ref, rows_in, nb)
    _load_time_major(ncst_ref, ext_ref, rows_in, CONV_B_PAD, nb)

    a_row = -jnp.exp(alog_ref[...])
    head_ok = lax.broadcasted_iota(jnp.int32, (1, LANES), 1) < HEAD_COPIES * N_HEADS
    a_row = jnp.where(head_ok, a_row, 0.0)
    causal = (lax.broadcasted_iota(jnp.int32, (q, q), 0)
              >= lax.broadcasted_iota(jnp.int32, (q, q), 1))
    not_causal = jnp.where(causal, 0.0, NEG_BIG)
    lane_lo = lax.broadcasted_iota(jnp.int32, (q, LANES), 1) < HEAD_DIM
    zero_b = jnp.zeros((q, LANES), BF16)

    for c0 in range(0, tm, q):
        rows = min(q, rows_in - c0)
        _conv_time_major(ext_ref, w_ref, b_ref, cout_ref, start=c0, n=q, nb=nb, taps=CONV_B_WIDTH,
                         lag0=CONV_B_PAD - (CONV_B_WIDTH - 1), steps=8, act=_silu)
        _load_time_major(xc_ref, cout_ref, c0, q, nb)

        if rows == q:
            dt = dt_ref[c0:c0 + q, :]
        else:
            dt = jnp.concatenate([dt_ref[c0:c0 + rows, :], jnp.zeros((q - rows, LANES), F32)],
                                 axis=0)
        acum = _cumsum_rows(dt * a_row)
        a_last = acum[q - 1:q, :]
        e_acum = jnp.exp(acum)
        s_tail = dt * jnp.exp(a_last - acum)
        acum2 = acum * LOG2_E
        col2 = acum2
        row2_t = (acum2 - jnp.log2(dt)).T

        terms = _split_terms(jnp.concatenate([e_acum, s_tail], axis=0))

        for g in range(N_GROUPS):
            gs = slice(g * GROUP_WIDTH, (g + 1) * GROUP_WIDTH)
            expanded = _dot(terms, expand_ref[:, gs])
            e_acum_x = expanded[0:q]
            s_tail_x = expanded[q:2 * q]
            e_last_x = expanded[q - 1:q]
            xs = xc_ref[:, gs]
            xs_b = xs.astype(BF16)
            b_g = xc_ref[:, D_INNER + g * D_STATE:D_INNER + (g + 1) * D_STATE]
            c_g = xc_ref[:, D_INNER + (N_GROUPS + g) * D_STATE:
                         D_INNER + (N_GROUPS + g + 1) * D_STATE]
            c_gb = c_g.astype(BF16)
            cb = lax.dot_general(c_gb, b_g.astype(BF16), (((1,), (1,)), ((), ())),
                                 preferred_element_type=F32)
            ht = ht_ref[g]
            y_st = _dot(c_gb, ht.astype(BF16)) * e_acum_x
            y_in = []
            for pr in range(HEADS_PER_GROUP // 2):
                ms = []
                for hh in range(2):
                    hd = g * HEADS_PER_GROUP + 2 * pr + hh
                    seg2 = col2[:, hd:hd + 1] - row2_t[hd:hd + 1, :] + not_causal
                    ms.append((cb * jnp.exp2(seg2)).astype(BF16))
                lhs = jnp.concatenate(ms, axis=1)
                xp = xs_b[:, pr * LANES:(pr + 1) * LANES]
                rhs = jnp.concatenate([jnp.where(lane_lo, xp, zero_b),
                                       jnp.where(lane_lo, zero_b, xp)], axis=0)
                y_in.append(_dot(lhs, rhs))
            ht_ref[g] = e_last_x * ht + _dot(b_g.T.astype(BF16), (xs * s_tail_x).astype(BF16))

            y = jnp.concatenate(y_in, axis=1) + y_st
            gated = ((y + dskip_ref[:, gs] * xs)[0:rows] * _silu(z_ref[c0:c0 + rows, gs]))
            ms = jnp.mean(gated * gated, axis=-1, keepdims=True)
            gn_ref[0:rows, gs] = (gated * lax.rsqrt(ms + RMS_EPS) * ng_ref[:, gs]).astype(BF16)

        y_ref[c0:c0 + rows, :] = _dot(gn_ref[0:rows, :], wout_ref[...])

    ext_ref[0:CONV_B_PAD * nb, :] = ext_ref[rows_in * nb:(rows_in + CONV_B_PAD) * nb, :]

    @pl.when(l == n_l - 1)
    def _():
        for g in range(N_GROUPS):
            nhst_ref[g] = ht_ref[g].T


def _ssd_call(xbc, z, dt, cst, hst, consts, *, tm):
    bsz, seq, _ = xbc.shape
    rows_in = min(tm, seq)
    tm_k = max(tm, SSD_CHUNK) if seq < SSD_CHUNK else tm
    n_l = seq // rows_in
    tile = lambda w_: pl.BlockSpec((None, rows_in, w_), lambda i, j: (i, j, 0))
    cstate = pl.BlockSpec((None, CONV_B_PAD, D_XBC), lambda i, j: (i, 0, 0))
    hstate = pl.BlockSpec((None, N_GROUPS, GROUP_WIDTH, D_STATE), lambda i, j: (i, 0, 0, 0))
    return pl.pallas_call(
        functools.partial(_ssd_kernel, tm=tm_k, rows_in=rows_in, n_l=n_l),
        grid=(bsz, n_l),
        in_specs=[tile(D_XBC), tile(D_INNER), tile(LANES), cstate, hstate]
        + [_whole()] * len(consts),
        out_specs=[tile(D_MODEL), cstate, hstate],
        out_shape=[jax.ShapeDtypeStruct((bsz, seq, D_MODEL), F32),
                   jax.ShapeDtypeStruct((bsz, CONV_B_PAD, D_XBC), F32),
                   jax.ShapeDtypeStruct((bsz, N_GROUPS, GROUP_WIDTH, D_STATE), F32)],
        scratch_shapes=[pltpu.VMEM(((tm_k + CONV_B_PAD) * (D_XBC // LANES), LANES), F32),
                        pltpu.VMEM((tm_k * (D_XBC // LANES), LANES), F32),
                        pltpu.VMEM((SSD_CHUNK, D_XBC), F32),
                        pltpu.VMEM((N_GROUPS, D_STATE, GROUP_WIDTH), F32),
                        pltpu.VMEM((SSD_CHUNK, D_INNER), BF16)],
        compiler_params=pltpu.CompilerParams(
            dimension_semantics=("arbitrary", "arbitrary"), vmem_limit_bytes=VMEM_LIMIT),
        name="ssd",
    )(xbc, z, dt, cst, hst, *consts)


def _pad_rows_front(a, rows):
    pad = rows - a.shape[-2]
    return jnp.pad(a, [(0, 0)] * (a.ndim - 2) + [(pad, 0), (0, 0)])


def _prepare_params(p):
    w_in = p["w_in"][0]
    s2 = 2 * D_CONV + D_INNER + D_XBC
    s3 = s2 + N_HEADS
    row = lambda v: v.reshape(1, -1).astype(F32)
    lane_pad = lambda v: jnp.pad(jnp.tile(v, (1, HEAD_COPIES)),
                                 ((0, 0), (0, LANES - HEAD_COPIES * v.shape[1])))
    lane_group = np.arange(LANES)[:, None]
    expand = ((lane_group % N_HEADS == np.arange(D_INNER)[None, :] // HEAD_DIM)
              & (lane_group < HEAD_COPIES * N_HEADS))
    q = {}
    for name in ("ffn1", "ffn2"):
        q[name] = (row(p["norm_" + name][0]), p[name + "_w_gate"][0].astype(BF16),
                   p[name + "_w_up"][0].astype(BF16), p[name + "_w_down"][0].astype(BF16))
    w_proj = jnp.concatenate([w_in[:, :s2], w_in[:, s3:], lane_pad(w_in[:, s2:s3])], axis=1)
    q["proj"] = (row(p["norm_mix"][0]), w_proj.astype(BF16), lane_pad(row(p["dt_bias"][0])))
    q["conv_a"] = (p["conv_a_w"][0].reshape(-1, LANES), p["conv_a_b"][0].reshape(-1, LANES),
                   row(p["ln_conv_g"][0]), row(p["ln_conv_b"][0]),
                   p["w_conv_a_out"][0].astype(BF16))
    q["ssd"] = (p["conv_b_w"][0].reshape(-1, LANES), p["conv_b_b"][0].reshape(-1, LANES),
                lane_pad(row(p["a_log"][0])),
                row(jnp.repeat(p["d_skip"][0], HEAD_DIM)), row(p["ssm_norm_g"][0]),
                jnp.asarray(expand, BF16), p["w_ssm_out"][0].astype(BF16))
    q["w_out"] = p["w_out"][0].astype(BF16)
    q["norm_final"] = row(p["norm_final"])
    return q


def _encoder(x, st_a, st_b, st_h, q, *, tm_ffn, tm_mix):
    bsz, seq, _ = x.shape
    t = bsz * seq
    x0 = x.reshape(t, D_MODEL)
    x1 = _ffn_call(x0, *q["ffn1"], tm=min(tm_ffn, t))
    tm_seq = min(tm_mix, seq)
    shp = lambda a: a.reshape(bsz, seq, a.shape[-1])
    st_a = _pad_rows_front(st_a, CONV_A_PAD)
    st_b = _pad_rows_front(st_b, CONV_B_PAD)
    st_h = st_h.reshape(bsz, N_GROUPS, GROUP_WIDTH, D_STATE)
    if seq >= tm_mix:
        m_a, new_a, z, xbc, dt, g_b = _proj_conv_call(shp(x1), st_a, q["proj"], q["conv_a"],
                                                      tm=tm_seq)
    else:
        u, z, xbc, dt, g_a, g_b = _proj_call(x1, *q["proj"], tm=min(tm_mix, t))
        m_a, new_a = _conv_a_call(shp(u), st_a, shp(g_a), *q["conv_a"], tm=tm_seq)
    y_b, new_b, new_h = _ssd_call(shp(xbc), shp(z), shp(dt), st_b, st_h, q["ssd"], tm=tm_seq)
    flat = lambda a: a.reshape(t, D_MODEL)
    y = _ffn_call(x1, *q["ffn2"], tm=min(tm_ffn, t),
                  merge=(flat(m_a), flat(y_b), flat(g_b), q["w_out"]), final_g=q["norm_final"])
    return (y.reshape(bsz, seq, D_MODEL),
            new_a[None, :, CONV_A_PAD - (CONV_A_WIDTH - 1):],
            new_b[None, :, CONV_B_PAD - (CONV_B_WIDTH - 1):],
            new_h.reshape(1, bsz, N_HEADS, HEAD_DIM, D_STATE))


def kernel(x_prompt, x_sample, state_conv_a, state_conv_b, state_ssm, norm_ffn1, ffn1_w_gate, ffn1_w_up, ffn1_w_down, norm_mix, w_in, conv_a_w, conv_a_b, ln_conv_g, ln_conv_b, w_conv_a_out, conv_b_w, conv_b_b, dt_bias, a_log, d_skip, ssm_norm_g, w_ssm_out, w_out, norm_ffn2, ffn2_w_gate, ffn2_w_up, ffn2_w_down, norm_final):
    p = dict(norm_ffn1=norm_ffn1, ffn1_w_gate=ffn1_w_gate, ffn1_w_up=ffn1_w_up,
             ffn1_w_down=ffn1_w_down, norm_mix=norm_mix, w_in=w_in, conv_a_w=conv_a_w,
             conv_a_b=conv_a_b, ln_conv_g=ln_conv_g, ln_conv_b=ln_conv_b,
             w_conv_a_out=w_conv_a_out, conv_b_w=conv_b_w, conv_b_b=conv_b_b, dt_bias=dt_bias,
             a_log=a_log, d_skip=d_skip, ssm_norm_g=ssm_norm_g, w_ssm_out=w_ssm_out, w_out=w_out,
             norm_ffn2=norm_ffn2, ffn2_w_gate=ffn2_w_gate, ffn2_w_up=ffn2_w_up,
             ffn2_w_down=ffn2_w_down, norm_final=norm_final)
    q = _prepare_params(p)
    bp = x_prompt.shape[0]
    zero_a = jnp.zeros((bp, CONV_A_WIDTH - 1, D_CONV), F32)
    zero_b = jnp.zeros((bp, CONV_B_WIDTH - 1, D_XBC), F32)
    zero_h = jnp.zeros((bp, N_HEADS, HEAD_DIM, D_STATE), F32)
    y_p, a_p, b_p, h_p = _encoder(x_prompt, zero_a, zero_b, zero_h, q, tm_ffn=512, tm_mix=256)
    y_s, a_s, b_s, h_s = _encoder(x_sample, state_conv_a[0], state_conv_b[0], state_ssm[0], q,
                                  tm_ffn=256, tm_mix=256)
    return (y_p, y_s, a_p, b_p, h_p, a_s, b_s, h_s)
```

```python
import functools

import jax
import jax.numpy as jnp
import numpy as np
from jax import lax
from jax.experimental import pallas as pl
from jax.experimental.pallas import tpu as pltpu

D_MODEL = 1024
D_FF = 2816
D_CONV = D_MODEL
CONV_A_WIDTH = 31
D_INNER = 2048
HEAD_DIM = 64
N_HEADS = D_INNER // HEAD_DIM
N_GROUPS = 4
HEADS_PER_GROUP = N_HEADS // N_GROUPS
GROUP_WIDTH = D_INNER // N_GROUPS
D_STATE = 128
CONV_B_WIDTH = 4
D_XBC = D_INNER + 2 * N_GROUPS * D_STATE
RMS_EPS = 1e-6
LN_EPS = 1e-5
FFN_RES_WEIGHT = 0.5

LANES = 128
SUBLANES = 8
CONV_A_PAD = 32
CONV_B_PAD = 8
SSD_CHUNK = 128
HEAD_COPIES = 3
FF_CHUNK = 256
PROJ_PANEL = 256
CONV_A_STEPS = 16
CONV_A_FINISH_ROWS = 256
VMEM_LIMIT = 56 * 1024 * 1024
NEG_BIG = -1e30
LOG2_E = 1.4426950408889634

F32 = jnp.float32
BF16 = jnp.bfloat16


def _dot(a, b):
    return jnp.dot(a, b, preferred_element_type=F32)


def _rms_norm(x, g):
    ms = jnp.mean(x * x, axis=-1, keepdims=True)
    return x * lax.rsqrt(ms + RMS_EPS) * g


def _sigmoid(x):
    return 0.5 * jnp.tanh(0.5 * x) + 0.5


def _silu(x):
    half = 0.5 * x
    return half + half * jnp.tanh(half)


def _whole():
    return pl.BlockSpec(memory_space=pltpu.VMEM)


def _ffn_kernel(*refs, merge, final):
    it = iter(refs)
    x_ref = next(it)
    if merge:
        ma_ref, yb_ref, gb_ref, wo_ref = next(it), next(it), next(it), next(it)
    ng_ref, wg_ref, wu_ref, wd_ref = next(it), next(it), next(it), next(it)
    if final:
        fg_ref = next(it)
    o_ref = next(it)
    a_scr = next(it)

    x = x_ref[...]
    if merge:
        merged = (ma_ref[...] + gb_ref[...] * yb_ref[...]).astype(BF16)
        x = x + _dot(merged, wo_ref[...])
    h = _rms_norm(x, ng_ref[...]).astype(BF16)
    for c in range(0, D_FF, FF_CHUNK):
        g = _dot(h, wg_ref[:, c:c + FF_CHUNK])
        u = _dot(h, wu_ref[:, c:c + FF_CHUNK])
        a_scr[:, c:c + FF_CHUNK] = (_silu(g) * u).astype(BF16)
    y = x + FFN_RES_WEIGHT * _dot(a_scr[...], wd_ref[...])
    if final:
        y = _rms_norm(y, fg_ref[...])
    o_ref[...] = y


def _ffn_call(x, ng, wg, wu, wd, *, tm, merge=None, final_g=None):
    t = x.shape[0]
    row = lambda w: pl.BlockSpec((tm, w), lambda i: (i, 0))
    args, specs = [x], [row(D_MODEL)]
    if merge is not None:
        ma, yb, gb, wo = merge
        args += [ma, yb, gb, wo]
        specs += [row(D_MODEL)] * 3 + [_whole()]
    args += [ng, wg, wu, wd]
    specs += [_whole()] * 4
    if final_g is not None:
        args.append(final_g)
        specs.append(_whole())
    return pl.pallas_call(
        functools.partial(_ffn_kernel, merge=merge is not None, final=final_g is not None),
        grid=(t // tm,),
        in_specs=specs,
        out_specs=row(D_MODEL),
        out_shape=jax.ShapeDtypeStruct((t, D_MODEL), F32),
        scratch_shapes=[pltpu.VMEM((tm, D_FF), BF16)],
        compiler_params=pltpu.CompilerParams(
            dimension_semantics=("arbitrary",), vmem_limit_bytes=VMEM_LIMIT),
        name="ffn_merge" if merge is not None else "ffn",
    )(*args)


PROJ_GLU = (0, 2 * D_CONV)
PROJ_Z = (PROJ_GLU[1], PROJ_GLU[1] + D_INNER)
PROJ_XBC = (PROJ_Z[1], PROJ_Z[1] + D_XBC)
PROJ_GATE = (PROJ_XBC[1], PROJ_XBC[1] + 2 * D_MODEL)
PROJ_DT = (PROJ_GATE[1], PROJ_GATE[1] + LANES)


def _project(h, w_ref, seg):
    return _dot(h, w_ref[:, seg[0]:seg[1]])


def _proj_kernel(x_ref, ng_ref, w_ref, dtb_ref, u_ref, z_ref, xbc_ref, dt_ref, ga_ref, gb_ref):
    h = _rms_norm(x_ref[...], ng_ref[...]).astype(BF16)
    glu = _project(h, w_ref, PROJ_GLU)
    u_ref[...] = glu[:, :D_CONV] * _sigmoid(glu[:, D_CONV:])
    z_ref[...] = _project(h, w_ref, PROJ_Z)
    xbc_ref[...] = _project(h, w_ref, PROJ_XBC)
    dt_ref[...] = jax.nn.softplus(_project(h, w_ref, PROJ_DT) + dtb_ref[...])
    gates = _sigmoid(_project(h, w_ref, PROJ_GATE))
    ga_ref[...] = gates[:, :D_MODEL]
    gb_ref[...] = gates[:, D_MODEL:]


def _proj_call(x, ng, w, dtb, *, tm):
    t = x.shape[0]
    row = lambda w_: pl.BlockSpec((tm, w_), lambda i: (i, 0))
    widths = (D_CONV, D_INNER, D_XBC, LANES, D_MODEL, D_MODEL)
    return pl.pallas_call(
        _proj_kernel,
        grid=(t // tm,),
        in_specs=[row(D_MODEL)] + [_whole()] * 3,
        out_specs=[row(w_) for w_ in widths],
        out_shape=[jax.ShapeDtypeStruct((t, w_), F32) for w_ in widths],
        compiler_params=pltpu.CompilerParams(
            dimension_semantics=("arbitrary",), vmem_limit_bytes=VMEM_LIMIT),
        name="in_proj",
    )(x, ng, w, dtb)


def _store_time_major(dst_ref, t0, src, n, nb):
    for j in range(nb):
        dst_ref[pl.ds(t0 * nb + j, n, stride=nb), :] = src[:, j * LANES:(j + 1) * LANES]


def _load_time_major(dst_ref, src_ref, t0, n, nb):
    for j in range(nb):
        dst_ref[:, j * LANES:(j + 1) * LANES] = src_ref[pl.ds(t0 * nb + j, n, stride=nb), :]


def _conv_time_major(ext_ref, w_ref, b_ref, out_ref, *, n, nb, taps, lag0, steps, act=None,
                     start=0):
    steps = min(steps, n)
    rep = lambda tile: jnp.concatenate([tile] * steps, axis=0)
    for t0 in range(start, start + n, steps):
        acc = rep(b_ref[...])
        for k in range(taps):
            lo = (t0 + lag0 + k) * nb
            acc = acc + rep(w_ref[k * nb:(k + 1) * nb, :]) * ext_ref[lo:lo + steps * nb, :]
        out_ref[t0 * nb:(t0 + steps) * nb, :] = acc if act is None else act(acc)


def _zero_after(x):
    bits = pltpu.bitcast(x[0:SUBLANES, 0:LANES], jnp.int32)
    return lax.shift_right_logical(lax.shift_right_logical(bits, 16), 16).astype(F32)


def _layer_norm_swish(v, g, b):
    mu = jnp.mean(v, axis=-1, keepdims=True)
    vc = v - mu
    var = jnp.mean(vc * vc, axis=-1, keepdims=True)
    return _silu(vc * lax.rsqrt(var + LN_EPS) * g + b)


def _conv_a_kernel(u_ref, st_ref, ga_ref, w_ref, b_ref, lng_ref, lnb_ref, wpw_ref,
                   ma_ref, nst_ref, ext_ref, out_ref, v_ref, *, tm):
    l = pl.program_id(1)
    nb = D_CONV // LANES

    @pl.when(l == 0)
    def _():
        _store_time_major(ext_ref, 0, st_ref, CONV_A_PAD, nb)

    _store_time_major(ext_ref, CONV_A_PAD, u_ref, tm, nb)
    _conv_time_major(ext_ref, w_ref, b_ref, out_ref, n=tm, nb=nb, taps=CONV_A_WIDTH,
                     lag0=CONV_A_PAD - (CONV_A_WIDTH - 1), steps=CONV_A_STEPS)
    _load_time_major(nst_ref, ext_ref, tm, CONV_A_PAD, nb)
    ext_ref[0:CONV_A_PAD * nb, :] = ext_ref[tm * nb:(tm + CONV_A_PAD) * nb, :]
    _load_time_major(v_ref, out_ref, 0, tm, nb)

    swished = _layer_norm_swish(v_ref[...], lng_ref[...], lnb_ref[...]).astype(BF16)
    ma_ref[...] = ga_ref[...] * _dot(swished, wpw_ref[...])


def _conv_a_call(u, st, ga, w, b, lng, lnb, wpw, *, tm):
    bsz, seq, _ = u.shape
    tile = lambda w_: pl.BlockSpec((None, tm, w_), lambda i, j: (i, j, 0))
    state = pl.BlockSpec((None, CONV_A_PAD, D_CONV), lambda i, j: (i, 0, 0))
    return pl.pallas_call(
        functools.partial(_conv_a_kernel, tm=tm),
        grid=(bsz, seq // tm),
        in_specs=[tile(D_CONV), state, tile(D_MODEL)] + [_whole()] * 5,
        out_specs=[tile(D_MODEL), state],
        out_shape=[jax.ShapeDtypeStruct((bsz, seq, D_MODEL), F32),
                   jax.ShapeDtypeStruct((bsz, CONV_A_PAD, D_CONV), F32)],
        scratch_shapes=[pltpu.VMEM(((tm + CONV_A_PAD) * (D_CONV // LANES), LANES), F32),
                        pltpu.VMEM((tm * (D_CONV // LANES), LANES), F32),
                        pltpu.VMEM((tm, D_CONV), F32)],
        compiler_params=pltpu.CompilerParams(
            dimension_semantics=("arbitrary", "arbitrary"), vmem_limit_bytes=VMEM_LIMIT),
        name="conv_a",
    )(u, st, ga, w, b, lng, lnb, wpw)


def _proj_conv_kernel(x_ref, st_ref, ng_ref, w_ref, dtb_ref, cw_ref, cb_ref, lng_ref, lnb_ref,
                      wpw_ref,
                      ma_ref, nst_ref, z_ref, xbc_ref, dt_ref, gb_ref,
                      ext_ref, out_ref, v_ref, h_ref, ga_ref, *, tm):
    l = pl.program_id(1)
    nb = D_CONV // LANES

    @pl.when(l == 0)
    def _():
        _store_time_major(ext_ref, 0, st_ref, CONV_A_PAD, nb)

    h = _rms_norm(x_ref[...], ng_ref[...]).astype(BF16)
    glu = _project(h, w_ref, PROJ_GLU)
    _store_time_major(ext_ref, CONV_A_PAD, glu[:, :D_CONV] * _sigmoid(glu[:, D_CONV:]), tm, nb)
    dt_ref[...] = jax.nn.softplus(_project(h, w_ref, PROJ_DT) + dtb_ref[...])
    h_ref[...] = h
    width = PROJ_PANEL
    panels = ([(ga_ref, PROJ_GATE[0] + c, c, _sigmoid)
               for c in range(0, D_MODEL, width)]
              + [(z_ref, PROJ_Z[0] + c, c, None) for c in range(0, D_INNER, width)]
              + [(xbc_ref, PROJ_XBC[0] + c, c, None) for c in range(0, D_XBC, width)]
              + [(gb_ref, PROJ_GATE[0] + D_MODEL + c, c, _sigmoid)
                 for c in range(0, D_MODEL, width)])
    steps = CONV_A_STEPS
    n_blocks = tm // steps
    finish_every = CONV_A_FINISH_ROWS // steps
    done = 0
    tail = jnp.zeros((SUBLANES, LANES), F32)
    for i in range(n_blocks):
        want = (len(panels) * (i + 1)) // n_blocks
        for dst, wc, oc, act in panels[done:want]:
            r = _project(h_ref[...], w_ref, (wc, wc + width))
            dst[:, oc:oc + width] = r if act is None else act(r)
            tail = tail + _zero_after(r)
        done = want
        block_tail = jnp.concatenate([tail] * (steps * nb // SUBLANES), axis=0)
        _conv_time_major(ext_ref, cw_ref, cb_ref, out_ref, start=i * steps, n=steps, nb=nb,
                         taps=CONV_A_WIDTH, lag0=CONV_A_PAD - (CONV_A_WIDTH - 1), steps=steps,
                         act=lambda a: a + block_tail)
        tail = jnp.zeros((SUBLANES, LANES), F32)
        if (i + 1) % finish_every == 0:
            r0 = (i + 1 - finish_every) * steps
            rows = slice(r0, r0 + CONV_A_FINISH_ROWS)
            for j in range(nb):
                v_ref[rows, j * LANES:(j + 1) * LANES] = out_ref[
                    pl.ds(r0 * nb + j, CONV_A_FINISH_ROWS, stride=nb), :]
            swished = _layer_norm_swish(v_ref[rows, :], lng_ref[...], lnb_ref[...]).astype(BF16)
            y_a = _dot(swished, wpw_ref[...])
            ma_ref[rows, :] = ga_ref[rows, :] * y_a
            tail = _zero_after(y_a)
    _load_time_major(nst_ref, ext_ref, tm, CONV_A_PAD, nb)
    ext_ref[0:CONV_A_PAD * nb, :] = ext_ref[tm * nb:(tm + CONV_A_PAD) * nb, :]


def _proj_conv_call(x, st, proj_w, conv_w, *, tm):
    bsz, seq, _ = x.shape
    nb = D_CONV // LANES
    tile = lambda w_: pl.BlockSpec((None, tm, w_), lambda i, j: (i, j, 0))
    state = pl.BlockSpec((None, CONV_A_PAD, D_CONV), lambda i, j: (i, 0, 0))
    widths = (D_MODEL, D_INNER, D_XBC, LANES, D_MODEL)
    outs = [jax.ShapeDtypeStruct((bsz, seq, w), F32) for w in widths]
    outs.insert(1, jax.ShapeDtypeStruct((bsz, CONV_A_PAD, D_CONV), F32))
    out_specs = [tile(w) for w in widths]
    out_specs.insert(1, state)
    return pl.pallas_call(
        functools.partial(_proj_conv_kernel, tm=tm),
        grid=(bsz, seq // tm),
        in_specs=[tile(D_MODEL), state] + [_whole()] * (len(proj_w) + len(conv_w)),
        out_specs=out_specs,
        out_shape=outs,
        scratch_shapes=[pltpu.VMEM(((tm + CONV_A_PAD) * nb, LANES), F32),
                        pltpu.VMEM((tm * nb, LANES), F32),
                        pltpu.VMEM((tm, D_CONV), F32),
                        pltpu.VMEM((tm, D_MODEL), BF16),
                        pltpu.VMEM((tm, D_MODEL), F32)],
        compiler_params=pltpu.CompilerParams(
            dimension_semantics=("arbitrary", "arbitrary"), vmem_limit_bytes=VMEM_LIMIT),
        name="proj_conv",
    )(x, st, *proj_w, *conv_w)


def _cumsum_rows(a):
    q = a.shape[0]
    row = lax.broadcasted_iota(jnp.int32, a.shape, 0)
    s = 1
    while s < q:
        a = a + jnp.where(row >= s, pltpu.roll(a, s, axis=0), 0.0)
        s *= 2
    return a


def _split_terms(v):
    lane = lax.broadcasted_iota(jnp.int32, v.shape, 1)
    hi = v.astype(BF16)
    rest = v - hi.astype(F32)
    mid = rest.astype(BF16)
    lo = (rest - mid.astype(F32)).astype(BF16)
    return jnp.where(lane < N_HEADS, hi, jnp.where(lane < 2 * N_HEADS, mid, lo))


def _ssd_kernel(xbc_ref, z_ref, dt_ref, cst_ref, hst_ref, w_ref, b_ref, alog_ref, dskip_ref,
                ng_ref, expand_ref, wout_ref,
                y_ref, ncst_ref, nhst_ref,
                ext_ref, cout_ref, xc_ref, ht_ref, gn_ref, *, tm, rows_in, n_l):
    l = pl.program_id(1)
    q = SSD_CHUNK
    nb = D_XBC // LANES

    @pl.when(l == 0)
    def _():
        _store_time_major(ext_ref, 0, cst_ref, CONV_B_PAD, nb)
        for g in range(N_GROUPS):
            ht_ref[g] = hst_ref[g].T

    if rows_in < tm:
        cout_ref[rows_in * nb:tm * nb, :] = jnp.zeros(((tm - rows_in) * nb, LANES), F32)
    _store_time_major(ext_ref, CONV_B_PAD, xbc_ref, rows_in, nb)
    _load_time_major(ncst_ref, ext_ref, rows_in, CONV_B_PAD, nb)

    a_row = -jnp.exp(alog_ref[...])
    head_ok = lax.broadcasted_iota(jnp.int32, (1, LANES), 1) < HEAD_COPIES * N_HEADS
    a_row = jnp.where(head_ok, a_row, 0.0)
    causal = (lax.broadcasted_iota(jnp.int32, (q, q), 0)
              >= lax.broadcasted_iota(jnp.int32, (q, q), 1))
    not_causal = jnp.where(causal, 0.0, NEG_BIG)
    lane_lo = lax.broadcasted_iota(jnp.int32, (q, LANES), 1) < HEAD_DIM
    zero_b = jnp.zeros((q, LANES), BF16)

    for c0 in range(0, tm, q):
        rows = min(q, rows_in - c0)
        _conv_time_major(ext_ref, w_ref, b_ref, cout_ref, start=c0, n=rows, nb=nb,
                         taps=CONV_B_WIDTH, lag0=CONV_B_PAD - (CONV_B_WIDTH - 1), steps=8,
                         act=_silu)
        _load_time_major(xc_ref, cout_ref, c0, q, nb)

        if rows == q:
            dt = dt_ref[c0:c0 + q, :]
        else:
            dt = jnp.concatenate([dt_ref[c0:c0 + rows, :], jnp.zeros((q - rows, LANES), F32)],
                                 axis=0)
        acum = _cumsum_rows(dt * a_row)
        a_last = acum[q - 1:q, :]
        e_acum = jnp.exp(acum)
        s_tail = dt * jnp.exp(a_last - acum)
        acum2 = acum * LOG2_E
        col2 = acum2
        row2_t = (acum2 - jnp.log2(dt)).T

        terms = _split_terms(jnp.concatenate([e_acum, s_tail], axis=0))

        for g in range(N_GROUPS):
            gs = slice(g * GROUP_WIDTH, (g + 1) * GROUP_WIDTH)
            expanded = _dot(terms, expand_ref[:, gs])
            e_acum_x = expanded[0:q]
            s_tail_x = expanded[q:2 * q]
            e_last_x = expanded[q - 1:q]
            xs = xc_ref[:, gs]
            xs_b = xs.astype(BF16)
            b_g = xc_ref[:, D_INNER + g * D_STATE:D_INNER + (g + 1) * D_STATE]
            c_g = xc_ref[:, D_INNER + (N_GROUPS + g) * D_STATE:
                         D_INNER + (N_GROUPS + g + 1) * D_STATE]
            c_gb = c_g.astype(BF16)
            cb = lax.dot_general(c_gb[0:rows], b_g.astype(BF16), (((1,), (1,)), ((), ())),
                                 preferred_element_type=F32)
            ht = ht_ref[g]
            y_st = _dot(c_gb[0:rows], ht.astype(BF16)) * e_acum_x[0:rows]
            y_in = []
            for pr in range(HEADS_PER_GROUP // 2):
                ms = []
                for hh in range(2):
                    hd = g * HEADS_PER_GROUP + 2 * pr + hh
                    seg2 = col2[0:rows, hd:hd + 1] - row2_t[hd:hd + 1, :] + not_causal[0:rows]
                    ms.append((cb * jnp.exp2(seg2)).astype(BF16))
                lhs = jnp.concatenate(ms, axis=1)
                xp = xs_b[:, pr * LANES:(pr + 1) * LANES]
                rhs = jnp.concatenate([jnp.where(lane_lo, xp, zero_b),
                                       jnp.where(lane_lo, zero_b, xp)], axis=0)
                y_in.append(_dot(lhs, rhs))
            ht_ref[g] = e_last_x * ht + _dot(b_g.T.astype(BF16), (xs * s_tail_x).astype(BF16))

            y = jnp.concatenate(y_in, axis=1) + y_st
            gated = (y + dskip_ref[:, gs] * xs[0:rows]) * _silu(z_ref[c0:c0 + rows, gs])
            ms = jnp.mean(gated * gated, axis=-1, keepdims=True)
            gn_ref[0:rows, gs] = (gated * lax.rsqrt(ms + RMS_EPS) * ng_ref[:, gs]).astype(BF16)

        y_ref[c0:c0 + rows, :] = _dot(gn_ref[0:rows, :], wout_ref[...])

    ext_ref[0:CONV_B_PAD * nb, :] = ext_ref[rows_in * nb:(rows_in + CONV_B_PAD) * nb, :]

    @pl.when(l == n_l - 1)
    def _():
        for g in range(N_GROUPS):
            nhst_ref[g] = ht_ref[g].T


def _ssd_call(xbc, z, dt, cst, hst, consts, *, tm):
    bsz, seq, _ = xbc.shape
    rows_in = min(tm, seq)
    tm_k = max(tm, SSD_CHUNK) if seq < SSD_CHUNK else tm
    n_l = seq // rows_in
    tile = lambda w_: pl.BlockSpec((None, rows_in, w_), lambda i, j: (i, j, 0))
    cstate = pl.BlockSpec((None, CONV_B_PAD, D_XBC), lambda i, j: (i, 0, 0))
    hstate = pl.BlockSpec((None, N_GROUPS, GROUP_WIDTH, D_STATE), lambda i, j: (i, 0, 0, 0))
    return pl.pallas_call(
        functools.partial(_ssd_kernel, tm=tm_k, rows_in=rows_in, n_l=n_l),
        grid=(bsz, n_l),
        in_specs=[tile(D_XBC), tile(D_INNER), tile(LANES), cstate, hstate]
        + [_whole()] * len(consts),
        out_specs=[tile(D_MODEL), cstate, hstate],
        out_shape=[jax.ShapeDtypeStruct((bsz, seq, D_MODEL), F32),
                   jax.ShapeDtypeStruct((bsz, CONV_B_PAD, D_XBC), F32),
                   jax.ShapeDtypeStruct((bsz, N_GROUPS, GROUP_WIDTH, D_STATE), F32)],
        scratch_shapes=[pltpu.VMEM(((tm_k + CONV_B_PAD) * (D_XBC // LANES), LANES), F32),
                        pltpu.VMEM((tm_k * (D_XBC // LANES), LANES), F32),
                        pltpu.VMEM((SSD_CHUNK, D_XBC), F32),
                        pltpu.VMEM((N_GROUPS, D_STATE, GROUP_WIDTH), F32),
                        pltpu.VMEM((SSD_CHUNK, D_INNER), BF16)],
        compiler_params=pltpu.CompilerParams(
            dimension_semantics=("arbitrary", "arbitrary"), vmem_limit_bytes=VMEM_LIMIT),
        name="ssd",
    )(xbc, z, dt, cst, hst, *consts)


def _pad_rows_front(a, rows):
    pad = rows - a.shape[-2]
    return jnp.pad(a, [(0, 0)] * (a.ndim - 2) + [(pad, 0), (0, 0)])


def _prepare_params(p):
    w_in = p["w_in"][0]
    s2 = 2 * D_CONV + D_INNER + D_XBC
    s3 = s2 + N_HEADS
    row = lambda v: v.reshape(1, -1).astype(F32)
    lane_pad = lambda v: jnp.pad(jnp.tile(v, (1, HEAD_COPIES)),
                                 ((0, 0), (0, LANES - HEAD_COPIES * v.shape[1])))
    lane_group = np.arange(LANES)[:, None]
    expand = ((lane_group % N_HEADS == np.arange(D_INNER)[None, :] // HEAD_DIM)
              & (lane_group < HEAD_COPIES * N_HEADS))
    q = {}
    for name in ("ffn1", "ffn2"):
        q[name] = (row(p["norm_" + name][0]), p[name + "_w_gate"][0].astype(BF16),
                   p[name + "_w_up"][0].astype(BF16), p[name + "_w_down"][0].astype(BF16))
    w_proj = jnp.concatenate([w_in[:, :s2], w_in[:, s3:], lane_pad(w_in[:, s2:s3])], axis=1)
    q["proj"] = (row(p["norm_mix"][0]), w_proj.astype(BF16), lane_pad(row(p["dt_bias"][0])))
    q["conv_a"] = (p["conv_a_w"][0].reshape(-1, LANES), p["conv_a_b"][0].reshape(-1, LANES),
                   row(p["ln_conv_g"][0]), row(p["ln_conv_b"][0]),
                   p["w_conv_a_out"][0].astype(BF16))
    q["ssd"] = (p["conv_b_w"][0].reshape(-1, LANES), p["conv_b_b"][0].reshape(-1, LANES),
                lane_pad(row(p["a_log"][0])),
                row(jnp.repeat(p["d_skip"][0], HEAD_DIM)), row(p["ssm_norm_g"][0]),
                jnp.asarray(expand, BF16), p["w_ssm_out"][0].astype(BF16))
    q["w_out"] = p["w_out"][0].astype(BF16)
    q["norm_final"] = row(p["norm_final"])
    return q


def _encoder(x, st_a, st_b, st_h, q, *, tm_ffn, tm_mix):
    bsz, seq, _ = x.shape
    t = bsz * seq
    x0 = x.reshape(t, D_MODEL)
    x1 = _ffn_call(x0, *q["ffn1"], tm=min(tm_ffn, t))
    tm_seq = min(tm_mix, seq)
    shp = lambda a: a.reshape(bsz, seq, a.shape[-1])
    st_a = _pad_rows_front(st_a, CONV_A_PAD)
    st_b = _pad_rows_front(st_b, CONV_B_PAD)
    st_h = st_h.reshape(bsz, N_GROUPS, GROUP_WIDTH, D_STATE)
    if seq >= tm_mix:
        m_a, new_a, z, xbc, dt, g_b = _proj_conv_call(shp(x1), st_a, q["proj"], q["conv_a"],
                                                      tm=tm_seq)
    else:
        u, z, xbc, dt, g_a, g_b = _proj_call(x1, *q["proj"], tm=min(tm_mix, t))
        m_a, new_a = _conv_a_call(shp(u), st_a, shp(g_a), *q["conv_a"], tm=tm_seq)
    y_b, new_b, new_h = _ssd_call(shp(xbc), shp(z), shp(dt), st_b, st_h, q["ssd"], tm=tm_seq)
    flat = lambda a: a.reshape(t, D_MODEL)
    y = _ffn_call(x1, *q["ffn2"], tm=min(tm_ffn, t),
                  merge=(flat(m_a), flat(y_b), flat(g_b), q["w_out"]), final_g=q["norm_final"])
    return (y.reshape(bsz, seq, D_MODEL),
            new_a[None, :, CONV_A_PAD - (CONV_A_WIDTH - 1):],
            new_b[None, :, CONV_B_PAD - (CONV_B_WIDTH - 1):],
            new_h.reshape(1, bsz, N_HEADS, HEAD_DIM, D_STATE))


def kernel(x_prompt, x_sample, state_conv_a, state_conv_b, state_ssm, norm_ffn1, ffn1_w_gate, ffn1_w_up, ffn1_w_down, norm_mix, w_in, conv_a_w, conv_a_b, ln_conv_g, ln_conv_b, w_conv_a_out, conv_b_w, conv_b_b, dt_bias, a_log, d_skip, ssm_norm_g, w_ssm_out, w_out, norm_ffn2, ffn2_w_gate, ffn2_w_up, ffn2_w_down, norm_final):
    p = dict(norm_ffn1=norm_ffn1, ffn1_w_gate=ffn1_w_gate, ffn1_w_up=ffn1_w_up,
             ffn1_w_down=ffn1_w_down, norm_mix=norm_mix, w_in=w_in, conv_a_w=conv_a_w,
             conv_a_b=conv_a_b, ln_conv_g=ln_conv_g, ln_conv_b=ln_conv_b,
             w_conv_a_out=w_conv_a_out, conv_b_w=conv_b_w, conv_b_b=conv_b_b, dt_bias=dt_bias,
             a_log=a_log, d_skip=d_skip, ssm_norm_g=ssm_norm_g, w_ssm_out=w_ssm_out, w_out=w_out,
             norm_ffn2=norm_ffn2, ffn2_w_gate=ffn2_w_gate, ffn2_w_up=ffn2_w_up,
             ffn2_w_down=ffn2_w_down, norm_final=norm_final)
    q = _prepare_params(p)
    bp = x_prompt.shape[0]
    zero_a = jnp.zeros((bp, CONV_A_WIDTH - 1, D_CONV), F32)
    zero_b = jnp.zeros((bp, CONV_B_WIDTH - 1, D_XBC), F32)
    zero_h = jnp.zeros((bp, N_HEADS, HEAD_DIM, D_STATE), F32)
    y_p, a_p, b_p, h_p = _encoder(x_prompt, zero_a, zero_b, zero_h, q, tm_ffn=512, tm_mix=256)
    y_s, a_s, b_s, h_s = _encoder(x_sample, state_conv_a[0], state_conv_b[0], state_ssm[0], q,
                                  tm_ffn=256, tm_mix=256)
    return (y_p, y_s, a_p, b_p, h_p, a_s, b_s, h_s)
```
